```python
import jax, jax.numpy as jnp
from jax import lax
import numpy as np

D_MODEL = 1024
BATCH = 16
SEQ = 2048
DEPTH = 2
DEC_BATCH = 32
DEC_SEQ = 2048
PAST_LEN = 128

D_MIX = D_MODEL
N_MIXERS = 4
GROUP_WIDTH = D_MIX // N_MIXERS
HEADS_PER_MIXER = 4
HEAD_DIM = GROUP_WIDTH // HEADS_PER_MIXER
CHUNK = 128
CONV_WIDTH = 3
POOL_WINDOWS = (2, 4, 8, 16)
D_IN = GROUP_WIDTH * 7
N_EXPERTS = 16
N_EXPERT_GROUPS = 4
EXPERTS_PER_GROUP = N_EXPERTS // N_EXPERT_GROUPS
TOP_K = 2
D_EXPERT = 512
N_MOD = 6
EPS = 1e-6

kernel_name = "hybrid_bidir_headgroup_moe_encoder"


def rmsnorm(x, g):
    xf = x.astype(jnp.float32)
    y = xf * lax.rsqrt(jnp.mean(xf * xf, axis=-1, keepdims=True) + EPS)
    return (y * g.astype(jnp.float32)).astype(x.dtype)


def fourier_mixer(a, w_f):
    b_, s_, _ = a.shape
    ah = a.reshape(b_, s_, HEADS_PER_MIXER, HEAD_DIM).astype(jnp.float32)
    f = jnp.fft.fft2(ah, axes=(1, 3), norm="ortho").real.astype(a.dtype)
    y = jnp.einsum('bshd,hde->bshe', f, w_f)
    return y.reshape(b_, s_, GROUP_WIDTH)


def spatial_gating(u, v, w_s, b_s):
    b_, s_, _ = v.shape
    n = s_ // CHUNK
    vh = v.reshape(b_, n, CHUNK, HEADS_PER_MIXER, HEAD_DIM)
    mixed = jnp.einsum('hpq,bnqhd->bnphd', w_s, vh) + b_s.T[None, None, :, :, None]
    return u * mixed.reshape(b_, s_, GROUP_WIDTH)


def short_gated_conv(bg, cg, hc, w_c):
    z = cg * hc
    s_ = z.shape[1]
    zp = jnp.pad(z, ((0, 0), (1, 1), (0, 0)))
    conv = zp[:, 0:s_] * w_c[0] + zp[:, 1:s_ + 1] * w_c[1] + zp[:, 2:s_ + 2] * w_c[2]
    return bg * conv


def multiscale_pool(p, w_p, p_scale):
    b_, s_, _ = p.shape
    pf = p.astype(jnp.float32)
    cs = jnp.pad(jnp.cumsum(pf, axis=1), ((0, 0), (1, 0), (0, 0)))
    t = jnp.arange(s_)
    outs = []
    for gi, w in enumerate(POOL_WINDOWS):
        left = w // 2
        right = w - 1 - left
        lo = jnp.clip(t - left, 0, s_)
        hi = jnp.clip(t + right + 1, 0, s_)
        csg = cs[:, :, gi * HEAD_DIM:(gi + 1) * HEAD_DIM]
        win_sum = csg[:, hi] - csg[:, lo]
        count = (hi - lo).astype(jnp.float32)[None, :, None]
        outs.append(win_sum / count - pf[:, :, gi * HEAD_DIM:(gi + 1) * HEAD_DIM])
    pooled = jnp.stack(outs, axis=2).astype(p.dtype)
    y = jnp.einsum('bsgd,gde->bsge', pooled, w_p).reshape(b_, s_, GROUP_WIDTH)
    return y * p_scale


def mixer_sublayer(h, w_in, w_f, w_s, b_s, w_c, w_p, p_scale, g_grp, w_out):
    G = GROUP_WIDTH
    proj = jnp.einsum('bsd,de->bse', h, w_in)
    y_a = fourier_mixer(proj[..., 0:G], w_f)
    y_b = spatial_gating(proj[..., G:2 * G], proj[..., 2 * G:3 * G], w_s, b_s)
    y_c = short_gated_conv(proj[..., 3 * G:4 * G], proj[..., 4 * G:5 * G], proj[..., 5 * G:6 * G], w_c)
    y_d = multiscale_pool(proj[..., 6 * G:7 * G], w_p, p_scale)
    y = jnp.stack([y_a, y_b, y_c, y_d], axis=2)
    y = rmsnorm(y, g_grp.reshape(N_MIXERS, G))
    b_, s_ = h.shape[:2]
    return jnp.einsum('bse,ed->bsd', y.reshape(b_, s_, D_MIX), w_out)


def route(x2, w_router, b_router):
    logits = jnp.einsum('td,de->te', x2, w_router).astype(jnp.float32)
    scores = jax.nn.softmax(logits, axis=-1)
    biased = scores + b_router.astype(jnp.float32)
    grouped = biased.reshape(-1, N_EXPERT_GROUPS, EXPERTS_PER_GROUP)
    group_score = lax.top_k(grouped, TOP_K)[0].sum(-1)
    sel = jnp.argmax(group_score, axis=-1)
    expert_group = jnp.arange(N_EXPERTS) // EXPERTS_PER_GROUP
    masked = jnp.where(expert_group[None, :] == sel[:, None], biased, -jnp.inf)
    _, idx = lax.top_k(masked, TOP_K)
    w = jnp.take_along_axis(scores, idx, axis=-1)
    w = w / jnp.sum(w, axis=-1, keepdims=True)
    return jnp.einsum('tk,tke->te', w, jax.nn.one_hot(idx, N_EXPERTS, dtype=jnp.float32))


def moe_sublayer(h, w_router, b_router, w_g, w_u, w_d):
    b_, s_, d_ = h.shape
    x2 = h.reshape(-1, d_)
    gates = route(x2, w_router, b_router).astype(h.dtype)

    def body(acc, xs):
        wg, wu, wd, g = xs
        y = jnp.dot(jax.nn.silu(jnp.dot(x2, wg)) * jnp.dot(x2, wu), wd)
        return acc + g[:, None] * y, None

    out, _ = lax.scan(body, jnp.zeros_like(x2), (w_g, w_u, w_d, gates.T))
    return out.reshape(b_, s_, d_)


def trunk(x, c, w_ada, b_ada, g_norm1, w_in, w_fourier, w_spatial, b_spatial, w_conv,
          w_pool, pool_scale, g_group, w_out, g_norm2, w_router, b_router,
          w_exp_gate, w_exp_up, w_exp_down, g_final):
    sc = jax.nn.silu(c)
    for l in range(DEPTH):
        mod = jnp.einsum('bd,de->be', sc, w_ada[l]) + b_ada[l]
        mod = mod.reshape(mod.shape[0], 1, N_MOD, D_MODEL)
        sh1, sc1, gt1 = mod[:, :, 0], mod[:, :, 1], mod[:, :, 2]
        sh2, sc2, gt2 = mod[:, :, 3], mod[:, :, 4], mod[:, :, 5]
        h = rmsnorm(x, g_norm1[l]) * (1 + sc1) + sh1
        x = x + gt1 * mixer_sublayer(h, w_in[l], w_fourier[l], w_spatial[l], b_spatial[l],
                                     w_conv[l], w_pool[l], pool_scale[l], g_group[l], w_out[l])
        h = rmsnorm(x, g_norm2[l]) * (1 + sc2) + sh2
        x = x + gt2 * moe_sublayer(h, w_router, b_router, w_exp_gate[l], w_exp_up[l], w_exp_down[l])
    return rmsnorm(x, g_final)


def setup_inputs(seed: int = 0) -> dict:
    key = jax.random.key(seed)
    ks = jax.random.split(key, 24)
    f32 = jnp.float32
    nrm = lambda k, shape, s: jax.random.normal(k, shape, f32) * s
    return {
        "x_prompt": nrm(ks[0], (BATCH, SEQ, D_MODEL), 1.0),
        "x_sample": nrm(ks[1], (DEC_BATCH, DEC_SEQ, D_MODEL), 1.0),
        "c_prompt": nrm(ks[2], (BATCH, D_MODEL), 1.0),
        "c_sample": nrm(ks[3], (DEC_BATCH, D_MODEL), 1.0),
        "w_ada": nrm(ks[4], (DEPTH, D_MODEL, N_MOD * D_MODEL), 0.5 * D_MODEL ** -0.5),
        "b_ada": nrm(ks[5], (DEPTH, N_MOD * D_MODEL), 0.02),
        "g_norm1": 1.0 + nrm(ks[6], (DEPTH, D_MODEL), 0.05),
        "w_in": nrm(ks[7], (DEPTH, D_MODEL, D_IN), D_MODEL ** -0.5),
        "w_fourier": nrm(ks[8], (DEPTH, HEADS_PER_MIXER, HEAD_DIM, HEAD_DIM), HEAD_DIM ** -0.5),
        "w_spatial": nrm(ks[9], (DEPTH, HEADS_PER_MIXER, CHUNK, CHUNK), CHUNK ** -0.5),
        "b_spatial": 1.0 + nrm(ks[10], (DEPTH, HEADS_PER_MIXER, CHUNK), 0.05),
        "w_conv": nrm(ks[11], (DEPTH, CONV_WIDTH, GROUP_WIDTH), CONV_WIDTH ** -0.5),
        "w_pool": nrm(ks[12], (DEPTH, len(POOL_WINDOWS), HEAD_DIM, HEAD_DIM), HEAD_DIM ** -0.5),
        "pool_scale": 1.0 + nrm(ks[13], (DEPTH, GROUP_WIDTH), 0.05),
        "g_group": 1.0 + nrm(ks[14], (DEPTH, D_MIX), 0.05),
        "w_out": nrm(ks[15], (DEPTH, D_MIX, D_MODEL), D_MIX ** -0.5),
        "g_norm2": 1.0 + nrm(ks[16], (DEPTH, D_MODEL), 0.05),
        "w_router": nrm(ks[17], (D_MODEL, N_EXPERTS), D_MODEL ** -0.5),
        "b_router": nrm(ks[18], (N_EXPERTS,), 0.01),
        "w_exp_gate": nrm(ks[19], (DEPTH, N_EXPERTS, D_MODEL, D_EXPERT), D_MODEL ** -0.5),
        "w_exp_up": nrm(ks[20], (DEPTH, N_EXPERTS, D_MODEL, D_EXPERT), D_MODEL ** -0.5),
        "w_exp_down": nrm(ks[21], (DEPTH, N_EXPERTS, D_EXPERT, D_MODEL), D_EXPERT ** -0.5),
        "g_final": 1.0 + nrm(ks[22], (D_MODEL,), 0.05),
    }


def reference(x_prompt, x_sample, c_prompt, c_sample, w_ada, b_ada, g_norm1, w_in, w_fourier,
              w_spatial, b_spatial, w_conv, w_pool, pool_scale, g_group, w_out, g_norm2,
              w_router, b_router, w_exp_gate, w_exp_up, w_exp_down, g_final):
    y_prompt = trunk(x_prompt, c_prompt, w_ada, b_ada, g_norm1, w_in, w_fourier, w_spatial,
                     b_spatial, w_conv, w_pool, pool_scale, g_group, w_out, g_norm2, w_router,
                     b_router, w_exp_gate, w_exp_up, w_exp_down, g_final)
    y_sample = trunk(x_sample, c_sample, w_ada, b_ada, g_norm1, w_in, w_fourier, w_spatial,
                     b_spatial, w_conv, w_pool, pool_scale, g_group, w_out, g_norm2, w_router,
                     b_router, w_exp_gate, w_exp_up, w_exp_down, g_final)
    return (y_prompt, y_sample)
```

```python
import functools

import numpy as np
import jax
import jax.numpy as jnp
from jax import lax
from jax.experimental import pallas as pl
from jax.experimental.pallas import tpu as pltpu

F32 = jnp.float32
BF16 = jnp.bfloat16

D_MODEL = 1024
GROUP = 256
N_HEADS = 4
HEAD_DIM = 64
CHUNK = 128
D_IN = 7 * GROUP
N_EXPERTS = 16
N_EGROUPS = 4
EXPERTS_PER_GROUP = 4
D_EXPERT = 512
N_MOD = 6
EPS = 1e-6
HALO = 8
PAIRS = ((0, 1), (0, 2), (0, 3), (1, 2), (1, 3), (2, 3))
N_CLASSES = N_EGROUPS * len(PAIRS)

TM_MIX = 512
TM_MOE = 512
VMEM_LIMIT = 56 * 1024 * 1024


def _rms(x):
    return x * lax.rsqrt(jnp.mean(x * x, axis=-1, keepdims=True) + EPS)


def _dot(a, b):
    return jnp.dot(a, b, preferred_element_type=F32)


def _ada_kernel(c_ref, w_ref, b_ref, o_ref):
    c = c_ref[...]
    sc = c * jax.nn.sigmoid(c)
    o_ref[0] = _dot(sc.astype(BF16), w_ref[0].astype(BF16)) + b_ref[0]


def _ada(c, w_ada, b_ada):
    depth, d, n = w_ada.shape
    bc = c.shape[0]
    tn = 1536
    return pl.pallas_call(
        _ada_kernel,
        grid=(depth, n // tn),
        in_specs=[
            pl.BlockSpec((bc, d), lambda l, j: (0, 0)),
            pl.BlockSpec((1, d, tn), lambda l, j: (l, 0, j)),
            pl.BlockSpec((1, 1, tn), lambda l, j: (l, 0, j)),
        ],
        out_specs=pl.BlockSpec((1, bc, tn), lambda l, j: (l, 0, j)),
        out_shape=jax.ShapeDtypeStruct((depth, bc, n), F32),
        compiler_params=pltpu.CompilerParams(
            dimension_semantics=("arbitrary", "arbitrary"), vmem_limit_bytes=VMEM_LIMIT),
        name="ada_mod",
    )(c, w_ada, b_ada.reshape(depth, 1, n))


def _mix_in_kernel(xm_ref, xp_ref, xn_ref, mod_ref, g1_ref, win_ref, d64_ref, ws_ref, bs_ref,
                   wc_ref, wp_ref, ps_ref, gg_ref, wo_ref, xg_ref, part_ref, *, tm, seq):
    i = pl.program_id(1)
    ne = tm + 2 * HALO
    xe = jnp.concatenate([xp_ref[0], xm_ref[0], xn_ref[0]], axis=0)
    mod = mod_ref[0]
    h = _rms(xe) * g1_ref[...] * (1.0 + mod[1:2]) + mod[0:1]
    proj = _dot(h.astype(BF16), win_ref[...])

    gpos = lax.broadcasted_iota(jnp.int32, (ne, GROUP), 0) + (i * tm - HALO)
    valid = (gpos >= 0) & (gpos < seq)
    pm = proj[HALO:HALO + tm]

    xg_ref[0] = _dot(pm[:, 0:GROUP].astype(BF16), d64_ref[...]).astype(BF16)

    u = pm[:, GROUP:2 * GROUP]
    v = pm[:, 2 * GROUP:3 * GROUP]
    head = lax.broadcasted_iota(jnp.int32, (CHUNK, GROUP), 1) // HEAD_DIM
    yb_chunks = []
    for c in range(tm // CHUNK):
        vc = v[c * CHUNK:(c + 1) * CHUNK].astype(BF16)
        m_all = _dot(ws_ref[...], vc)
        mixed = bs_ref[...]
        for hh in range(N_HEADS):
            mixed = mixed + jnp.where(head == hh, m_all[hh * CHUNK:(hh + 1) * CHUNK], 0.0)
        yb_chunks.append(u[c * CHUNK:(c + 1) * CHUNK] * mixed)
    yb = jnp.concatenate(yb_chunks, axis=0)

    z = jnp.where(valid, proj[:, 4 * GROUP:5 * GROUP] * proj[:, 5 * GROUP:6 * GROUP], 0.0)
    conv = (pltpu.roll(z, 1, 0)[HALO:HALO + tm] * wc_ref[0:1, :]
            + z[HALO:HALO + tm] * wc_ref[1:2, :]
            + pltpu.roll(z, ne - 1, 0)[HALO:HALO + tm] * wc_ref[2:3, :])
    yc = pm[:, 3 * GROUP:4 * GROUP] * conv

    p = jnp.where(valid, proj[:, 6 * GROUP:7 * GROUP], 0.0)
    a2 = p + pltpu.roll(p, 1, 0)
    a4 = a2 + pltpu.roll(a2, 2, 0)
    a8 = a4 + pltpu.roll(a4, 4, 0)
    a16 = a8 + pltpu.roll(a8, 8, 0)
    w2 = a2[HALO:HALO + tm]
    w4 = pltpu.roll(a4, ne - 1, 0)[HALO:HALO + tm]
    w8 = pltpu.roll(a8, ne - 3, 0)[HALO:HALO + tm]
    w16 = pltpu.roll(a16, ne - 7, 0)[HALO:HALO + tm]
    grp = lax.broadcasted_iota(jnp.int32, (tm, GROUP), 1) // HEAD_DIM
    wsum = jnp.where(grp == 0, w2, jnp.where(grp == 1, w4, jnp.where(grp == 2, w8, w16)))
    left = jnp.left_shift(1, grp)
    t = lax.broadcasted_iota(jnp.int32, (tm, GROUP), 0) + i * tm
    cnt = jnp.minimum(t + left, seq) - jnp.maximum(t - left, 0)
    pooled = wsum / cnt.astype(F32) - p[HALO:HALO + tm]
    yd = _dot(pooled.astype(BF16), wp_ref[...]) * ps_ref[...]

    gg = gg_ref[...]
    ycat = jnp.concatenate([
        (_rms(yb) * gg[:, GROUP:2 * GROUP]).astype(BF16),
        (_rms(yc) * gg[:, 2 * GROUP:3 * GROUP]).astype(BF16),
        (_rms(yd) * gg[:, 3 * GROUP:4 * GROUP]).astype(BF16)], axis=1)
    part_ref[0] = _dot(ycat, wo_ref[...])


def _mix_in(x, mod, g1, w_in, d64, ws, bs, wc, wp, ps, gg, wo_bcd):
    b, seq, d = x.shape
    tm = TM_MIX
    nt = seq // tm
    hb = tm // HALO
    full = lambda *shape: pl.BlockSpec(shape, lambda bi, i: (0,) * len(shape))
    return pl.pallas_call(
        functools.partial(_mix_in_kernel, tm=tm, seq=seq),
        grid=(b, nt),
        in_specs=[
            pl.BlockSpec((1, tm, d), lambda bi, i: (bi, i, 0)),
            pl.BlockSpec((1, HALO, d), lambda bi, i: (bi, jnp.maximum(i * hb - 1, 0), 0)),
            pl.BlockSpec((1, HALO, d), lambda bi, i: (bi, jnp.minimum((i + 1) * hb, seq // HALO - 1), 0)),
            pl.BlockSpec((1, N_MOD, d), lambda bi, i: (bi, 0, 0)),
            full(1, d), full(d, D_IN), full(GROUP, 2 * GROUP), full(N_HEADS * CHUNK, CHUNK),
            full(CHUNK, GROUP), full(3, GROUP), full(GROUP, GROUP), full(1, GROUP),
            full(1, d), full(3 * GROUP, d),
        ],
        out_specs=[
            pl.BlockSpec((1, tm, 2 * GROUP), lambda bi, i: (bi, i, 0)),
            pl.BlockSpec((1, tm, d), lambda bi, i: (bi, i, 0)),
        ],
        out_shape=[
            jax.ShapeDtypeStruct((b, seq, 2 * GROUP), BF16),
            jax.ShapeDtypeStruct((b, seq, d), F32),
        ],
        compiler_params=pltpu.CompilerParams(
            dimension_semantics=("arbitrary", "arbitrary"), vmem_limit_bytes=VMEM_LIMIT),
        name="mix_in",
    )(x, x, x, mod, g1, w_in, d64, ws, bs, wc, wp, ps, gg, wo_bcd)


def _top2_of4(vals, aux):
    best, bi, ba = vals[0], jnp.zeros_like(vals[0], jnp.int32), aux[0]
    for j in range(1, 4):
        gt = vals[j] > best
        best = jnp.where(gt, vals[j], best)
        bi = jnp.where(gt, j, bi)
        ba = jnp.where(gt, aux[j], ba)
    sec = jnp.full_like(best, -jnp.inf)
    si, sa = jnp.zeros_like(bi), aux[0]
    for j in range(4):
        cand = jnp.where(bi == j, -jnp.inf, vals[j])
        gt = cand > sec
        sec = jnp.where(gt, cand, sec)
        si = jnp.where(gt, j, si)
        sa = jnp.where(gt, aux[j], sa)
    return best, sec, bi, si, ba, sa


def _mix_out_kernel(cs_ref, ss_ref, xg_ref, part_ref, x_ref, mod_ref, wf_ref, gg_ref, wo_ref,
                    g2_ref, wrh_ref, wrl_ref, br_ref, xo_ref, h2_ref, rt_ref):
    xg = xg_ref[0]
    f = _dot(cs_ref[...], xg[:, 0:GROUP]) + _dot(ss_ref[...], xg[:, GROUP:2 * GROUP])
    ya = _dot(f.astype(BF16), wf_ref[...])
    ya = _rms(ya) * gg_ref[:, 0:GROUP]
    mix = _dot(ya.astype(BF16), wo_ref[...]) + part_ref[0]
    mod = mod_ref[0]
    xn = x_ref[0] + mod[2:3] * mix
    xo_ref[0] = xn
    h2 = _rms(xn) * g2_ref[...] * (1.0 + mod[4:5]) + mod[3:4]
    h_hi = h2.astype(BF16)
    h2_ref[0] = h_hi
    h_lo = (h2 - h_hi.astype(F32)).astype(BF16)

    nt_dims = (((1,), (1,)), ((), ()))
    w_hi, w_lo = wrh_ref[...], wrl_ref[...]
    lt = (lax.dot_general(w_hi, h_hi, nt_dims, preferred_element_type=F32)
          + lax.dot_general(w_hi, h_lo, nt_dims, preferred_element_type=F32)
          + lax.dot_general(w_lo, h_hi, nt_dims, preferred_element_type=F32))
    ex = jnp.exp(lt - jnp.max(lt, axis=0, keepdims=True))
    scores = ex / jnp.sum(ex, axis=0, keepdims=True)
    biased = scores + br_ref[...]

    best_gs = None
    for g in range(N_EGROUPS):
        rows = [biased[g * 4 + j:g * 4 + j + 1] for j in range(4)]
        srow = [scores[g * 4 + j:g * 4 + j + 1] for j in range(4)]
        t1, t2, i1, i2, s1, s2 = _top2_of4(rows, srow)
        gs = t1 + t2
        if best_gs is None:
            best_gs, sel = gs, jnp.zeros_like(i1)
            bi1, bi2, bs1, bs2 = i1, i2, s1, s2
        else:
            gt = gs > best_gs
            best_gs = jnp.where(gt, gs, best_gs)
            sel = jnp.where(gt, g, sel)
            bi1, bi2 = jnp.where(gt, i1, bi1), jnp.where(gt, i2, bi2)
            bs1, bs2 = jnp.where(gt, s1, bs1), jnp.where(gt, s2, bs2)
    first_lo = bi1 < bi2
    lo = jnp.where(first_lo, bi1, bi2)
    hi = jnp.where(first_lo, bi2, bi1)
    den = bs1 + bs2
    w_a = jnp.where(first_lo, bs1, bs2) / den
    w_b = jnp.where(first_lo, bs2, bs1) / den
    pair = jnp.where(lo == 0, hi - 1, jnp.where(lo == 1, hi + 1, 5))
    cls = (sel * len(PAIRS) + pair).astype(F32)
    tk = cls.shape[1]
    rt_ref[0] = jnp.concatenate([cls, w_a, w_b, jnp.zeros((5, tk), F32)], axis=0)


def _mix_out(cs, ss, xg, part, x, mod, wf_bd, gg, wo_a, g2, wr_hi, wr_lo, b_router):
    b, seq, d = x.shape
    tk = TM_MIX
    nk = seq // tk
    full = lambda *shape: pl.BlockSpec(shape, lambda k, bi: (0,) * len(shape))
    return pl.pallas_call(
        _mix_out_kernel,
        grid=(nk, b),
        in_specs=[
            pl.BlockSpec((tk, seq), lambda k, bi: (k, 0)),
            pl.BlockSpec((tk, seq), lambda k, bi: (k, 0)),
            pl.BlockSpec((1, seq, 2 * GROUP), lambda k, bi: (bi, 0, 0)),
            pl.BlockSpec((1, tk, d), lambda k, bi: (bi, k, 0)),
            pl.BlockSpec((1, tk, d), lambda k, bi: (bi, k, 0)),
            pl.BlockSpec((1, N_MOD, d), lambda k, bi: (bi, 0, 0)),
            full(GROUP, GROUP), full(1, d), full(GROUP, d), full(1, d),
            full(N_EXPERTS, d), full(N_EXPERTS, d), full(N_EXPERTS, 1),
        ],
        out_specs=[
            pl.BlockSpec((1, tk, d), lambda k, bi: (bi, k, 0)),
            pl.BlockSpec((1, tk, d), lambda k, bi: (bi, k, 0)),
            pl.BlockSpec((1, 8, tk), lambda k, bi: (bi, 0, k)),
        ],
        out_shape=[
            jax.ShapeDtypeStruct((b, seq, d), F32),
            jax.ShapeDtypeStruct((b, seq, d), BF16),
            jax.ShapeDtypeStruct((b, 8, seq), F32),
        ],
        compiler_params=pltpu.CompilerParams(
            dimension_semantics=("arbitrary", "arbitrary"), vmem_limit_bytes=VMEM_LIMIT),
        name="mix_out",
    )(cs, ss, xg, part, x, mod, wf_bd, gg, wo_a, g2, wr_hi, wr_lo, b_router)


def _moe_kernel(ea_ref, eb_ref, nu_ref, xs_ref, wt_ref, wga_ref, wua_ref, wda_ref,
                wgb_ref, wub_ref, wdb_ref, o_ref):
    @pl.when(pl.program_id(0) < nu_ref[0])
    def _():
        xs = xs_ref[...]
        wt = wt_ref[...]

        def ffn(wg, wu, wd):
            a = _dot(xs, wg[0])
            hid = (a * jax.nn.sigmoid(a)) * _dot(xs, wu[0])
            return _dot(hid.astype(BF16), wd[0])

        o_ref[...] = (wt[:, 0:1] * ffn(wga_ref, wua_ref, wda_ref)
                      + wt[:, 1:2] * ffn(wgb_ref, wub_ref, wdb_ref))


def _moe_ffn(tile_ea, tile_eb, n_used, xs, wts, wg, wu, wd):
    p, d = xs.shape
    tm = TM_MOE
    nt = p // tm
    row = lambda i, ea, eb, nu: (jnp.minimum(i, nu[0] - 1), 0)
    exp_a = lambda i, ea, eb, nu: (ea[i], 0, 0)
    exp_b = lambda i, ea, eb, nu: (eb[i], 0, 0)
    return pl.pallas_call(
        _moe_kernel,
        grid_spec=pltpu.PrefetchScalarGridSpec(
            num_scalar_prefetch=3,
            grid=(nt,),
            in_specs=[
                pl.BlockSpec((tm, d), row),
                pl.BlockSpec((tm, 2), row),
                pl.BlockSpec((1, d, D_EXPERT), exp_a),
                pl.BlockSpec((1, d, D_EXPERT), exp_a),
                pl.BlockSpec((1, D_EXPERT, d), exp_a),
                pl.BlockSpec((1, d, D_EXPERT), exp_b),
                pl.BlockSpec((1, d, D_EXPERT), exp_b),
                pl.BlockSpec((1, D_EXPERT, d), exp_b),
            ],
            out_specs=pl.BlockSpec((tm, d), row),
        ),
        out_shape=jax.ShapeDtypeStruct((p, d), F32),
        compiler_params=pltpu.CompilerParams(
            dimension_semantics=("arbitrary",), vmem_limit_bytes=VMEM_LIMIT),
        name="moe_ffn",
    )(tile_ea, tile_eb, n_used, xs, wts, wg, wu, wd, wg, wu, wd)


def _combine_kernel(x_ref, y_ref, mod_ref, gf_ref, o_ref, *, final):
    xn = x_ref[0] + mod_ref[0][5:6] * y_ref[0]
    if final:
        xn = _rms(xn) * gf_ref[...]
    o_ref[0] = xn


def _combine(x, y, mod, g_final, final):
    b, seq, d = x.shape
    tm = TM_MIX
    blk = pl.BlockSpec((1, tm, d), lambda bi, i: (bi, i, 0))
    return pl.pallas_call(
        functools.partial(_combine_kernel, final=final),
        grid=(b, seq // tm),
        in_specs=[blk, blk, pl.BlockSpec((1, N_MOD, d), lambda bi, i: (bi, 0, 0)),
                  pl.BlockSpec((1, d), lambda bi, i: (0, 0))],
        out_specs=blk,
        out_shape=jax.ShapeDtypeStruct((b, seq, d), F32),
        compiler_params=pltpu.CompilerParams(
            dimension_semantics=("arbitrary", "arbitrary"), vmem_limit_bytes=VMEM_LIMIT),
        name="combine",
    )(x, y, mod, g_final)


def _dft_tables(seq):
    n = np.arange(seq)
    ang = 2.0 * np.pi * ((n[:, None] * n[None, :]) % seq) / seq
    scale = 1.0 / np.sqrt(seq)
    return (np.cos(ang) * scale).astype(np.float32), (-np.sin(ang) * scale).astype(np.float32)


def _dft64_blockdiag():
    n = np.arange(HEAD_DIM)
    ang = 2.0 * np.pi * ((n[:, None] * n[None, :]) % HEAD_DIM) / HEAD_DIM
    c, s = np.cos(ang) / np.sqrt(HEAD_DIM), np.sin(ang) / np.sqrt(HEAD_DIM)
    bd = np.zeros((GROUP, 2 * GROUP), np.float32)
    for h in range(N_HEADS):
        r = slice(h * HEAD_DIM, (h + 1) * HEAD_DIM)
        bd[r, h * HEAD_DIM:(h + 1) * HEAD_DIM] = c
        bd[r, GROUP + h * HEAD_DIM:GROUP + (h + 1) * HEAD_DIM] = s
    return bd


def _block_diag(w):
    n, k, _ = w.shape
    eye = jnp.eye(n, dtype=w.dtype)
    return (eye[:, None, :, None] * w[:, :, None, :]).reshape(n * k, n * k)


def _route_tables(cls, tm):
    t = cls.shape[0]
    nt = t // tm + N_CLASSES
    onehot = (cls[:, None] == jnp.arange(N_CLASSES, dtype=jnp.int32)[None, :]).astype(jnp.int32)
    csum = jnp.cumsum(onehot, axis=0)
    rank = jnp.take_along_axis(csum, cls[:, None], axis=1)[:, 0] - 1
    counts = csum[-1]
    ntile_c = (counts + tm - 1) // tm
    tile_end = jnp.cumsum(ntile_c)
    pstart = (tile_end - ntile_c) * tm
    inv = pstart[cls] + rank
    n_used = tile_end[-1]
    tile_ids = jnp.minimum(jnp.arange(nt, dtype=jnp.int32), n_used - 1)
    tile_cls = jnp.searchsorted(tile_end, tile_ids, side="right").astype(jnp.int32)
    pair_lo = jnp.array([p[0] for p in PAIRS], jnp.int32)
    pair_hi = jnp.array([p[1] for p in PAIRS], jnp.int32)
    grp, pid = tile_cls // len(PAIRS), tile_cls % len(PAIRS)
    tile_ea = grp * EXPERTS_PER_GROUP + pair_lo[pid]
    tile_eb = grp * EXPERTS_PER_GROUP + pair_hi[pid]
    return inv, tile_ea, tile_eb, n_used.reshape(1).astype(jnp.int32), nt


def _trunk(x, mods, lw, shared):
    b, seq, d = x.shape
    t = b * seq
    depth = len(lw)
    for l in range(depth):
        w = lw[l]
        mod = mods[l]
        xg, part = _mix_in(x, mod, w["g1"], w["w_in"], shared["d64"], w["ws"], w["bs"], w["wc"],
                           w["wp"], w["ps"], w["gg"], w["wo_bcd"])
        x_mid, h2, route = _mix_out(shared["cs"], shared["ss"], xg, part, x, mod, w["wf"], w["gg"],
                                    w["wo_a"], w["g2"], shared["wr_hi"], shared["wr_lo"],
                                    shared["b_router"])
        cls = route[:, 0, :].reshape(t).astype(jnp.int32)
        wts = jnp.stack([route[:, 1, :].reshape(t), route[:, 2, :].reshape(t)], axis=1)
        inv, tile_ea, tile_eb, n_used, nt = _route_tables(cls, TM_MOE)
        p = nt * TM_MOE
        src = jnp.zeros((p,), jnp.int32).at[inv].set(jnp.arange(t, dtype=jnp.int32))
        wts_s = jnp.zeros((p, 2), F32).at[inv].set(wts)
        xs = jnp.take(h2.reshape(t, d), src, axis=0)
        ys = _moe_ffn(tile_ea, tile_eb, n_used, xs, wts_s, w["wg"], w["wu"], w["wd"])
        y_tok = jnp.take(ys, inv, axis=0).reshape(b, seq, d)
        x = _combine(x_mid, y_tok, mod, shared["g_final"], final=(l == depth - 1))
    return x


def kernel(x_prompt, x_sample, c_prompt, c_sample, w_ada, b_ada, g_norm1, w_in, w_fourier, w_spatial, b_spatial, w_conv, w_pool, pool_scale, g_group, w_out, g_norm2, w_router, b_router, w_exp_gate, w_exp_up, w_exp_down, g_final):
    depth = w_in.shape[0]
    seq = x_prompt.shape[1]
    d = D_MODEL
    nb_p = c_prompt.shape[0]

    cs, ss = _dft_tables(seq)
    wr_hi = w_router.astype(BF16)
    wr_lo = (w_router - wr_hi.astype(F32)).astype(BF16)
    shared = {
        "cs": jnp.asarray(cs).astype(BF16),
        "ss": jnp.asarray(ss).astype(BF16),
        "d64": jnp.asarray(_dft64_blockdiag()).astype(BF16),
        "wr_hi": wr_hi.T, "wr_lo": wr_lo.T,
        "b_router": b_router.reshape(N_EXPERTS, 1).astype(F32),
        "g_final": g_final.reshape(1, d),
    }
    lw = []
    for l in range(depth):
        lw.append({
            "g1": g_norm1[l].reshape(1, d),
            "w_in": w_in[l].astype(BF16),
            "ws": w_spatial[l].reshape(N_HEADS * CHUNK, CHUNK).astype(BF16),
            "bs": jnp.repeat(b_spatial[l].T, HEAD_DIM, axis=1),
            "wc": w_conv[l],
            "wp": _block_diag(w_pool[l]).astype(BF16),
            "ps": pool_scale[l].reshape(1, GROUP),
            "gg": g_group[l].reshape(1, d),
            "wo_bcd": w_out[l, GROUP:].astype(BF16),
            "wo_a": w_out[l, :GROUP].astype(BF16),
            "wf": _block_diag(w_fourier[l]).astype(BF16),
            "g2": g_norm2[l].reshape(1, d),
            "wg": w_exp_gate[l].astype(BF16),
            "wu": w_exp_up[l].astype(BF16),
            "wd": w_exp_down[l].astype(BF16),
        })

    c_all = jnp.concatenate([c_prompt, c_sample], axis=0)
    mod_all = _ada(c_all, w_ada, b_ada).reshape(depth, c_all.shape[0], N_MOD, d)
    mods_p = [mod_all[l, :nb_p] for l in range(depth)]
    mods_s = [mod_all[l, nb_p:] for l in range(depth)]
    y_prompt = _trunk(x_prompt, mods_p, lw, shared)
    y_sample = _trunk(x_sample, mods_s, lw, shared)
    return (y_prompt, y_sample)
```

```python
import functools

import numpy as np
import jax
import jax.numpy as jnp
from jax import lax
from jax.experimental import pallas as pl
from jax.experimental.pallas import tpu as pltpu

F32 = jnp.float32
BF16 = jnp.bfloat16

D_MODEL = 1024
GROUP = 256
N_HEADS = 4
HEAD_DIM = 64
CHUNK = 128
D_IN = 7 * GROUP
N_EXPERTS = 16
N_EGROUPS = 4
EXPERTS_PER_GROUP = 4
D_EXPERT = 512
N_MOD = 6
EPS = 1e-6
HALO = 8
PAIRS = ((0, 1), (0, 2), (0, 3), (1, 2), (1, 3), (2, 3))
N_CLASSES = N_EGROUPS * len(PAIRS)
CLS_PAD = 32

TM_MIX = 512
TM_MOE = 512
VMEM_LIMIT = 56 * 1024 * 1024


def _rms(x):
    return x * lax.rsqrt(jnp.mean(x * x, axis=-1, keepdims=True) + EPS)


def _dot(a, b):
    return jnp.dot(a, b, preferred_element_type=F32)


def _ada_kernel(c_ref, w_ref, b_ref, o_ref):
    c = c_ref[...]
    sc = c * jax.nn.sigmoid(c)
    o_ref[0] = _dot(sc.astype(BF16), w_ref[0].astype(BF16)) + b_ref[0]


def _ada(c, w_ada, b_ada):
    depth, d, n = w_ada.shape
    bc = c.shape[0]
    tn = 1536
    return pl.pallas_call(
        _ada_kernel,
        grid=(depth, n // tn),
        in_specs=[
            pl.BlockSpec((bc, d), lambda l, j: (0, 0)),
            pl.BlockSpec((1, d, tn), lambda l, j: (l, 0, j)),
            pl.BlockSpec((1, 1, tn), lambda l, j: (l, 0, j)),
        ],
        out_specs=pl.BlockSpec((1, bc, tn), lambda l, j: (l, 0, j)),
        out_shape=jax.ShapeDtypeStruct((depth, bc, n), F32),
        compiler_params=pltpu.CompilerParams(
            dimension_semantics=("arbitrary", "arbitrary"), vmem_limit_bytes=VMEM_LIMIT),
        name="ada_mod",
    )(c, w_ada, b_ada.reshape(depth, 1, n))


def _mix_in_kernel(xm_ref, xp_ref, xn_ref, mod_ref, g1_ref, win_ref, d64_ref, ws_ref, bs_ref,
                   wc_ref, wp_ref, ps_ref, gg_ref, wo_ref, xg_ref, part_ref, *, tm, seq):
    i = pl.program_id(1)
    ne = tm + 2 * HALO
    xe = jnp.concatenate([xp_ref[0], xm_ref[0], xn_ref[0]], axis=0)
    mod = mod_ref[0]
    h = _rms(xe) * g1_ref[...] * (1.0 + mod[1:2]) + mod[0:1]
    proj = _dot(h.astype(BF16), win_ref[...])

    gpos = lax.broadcasted_iota(jnp.int32, (ne, GROUP), 0) + (i * tm - HALO)
    valid = (gpos >= 0) & (gpos < seq)
    pm = proj[HALO:HALO + tm]

    xg_ref[0] = _dot(pm[:, 0:GROUP].astype(BF16), d64_ref[...]).astype(BF16)

    u = pm[:, GROUP:2 * GROUP]
    v = pm[:, 2 * GROUP:3 * GROUP]
    head = lax.broadcasted_iota(jnp.int32, (CHUNK, GROUP), 1) // HEAD_DIM
    yb_chunks = []
    for c in range(tm // CHUNK):
        vc = v[c * CHUNK:(c + 1) * CHUNK].astype(BF16)
        m_all = _dot(ws_ref[...], vc)
        mixed = bs_ref[...]
        for hh in range(N_HEADS):
            mixed = mixed + jnp.where(head == hh, m_all[hh * CHUNK:(hh + 1) * CHUNK], 0.0)
        yb_chunks.append(u[c * CHUNK:(c + 1) * CHUNK] * mixed)
    yb = jnp.concatenate(yb_chunks, axis=0)

    z = jnp.where(valid, proj[:, 4 * GROUP:5 * GROUP] * proj[:, 5 * GROUP:6 * GROUP], 0.0)
    conv = (pltpu.roll(z, 1, 0)[HALO:HALO + tm] * wc_ref[0:1, :]
            + z[HALO:HALO + tm] * wc_ref[1:2, :]
            + pltpu.roll(z, ne - 1, 0)[HALO:HALO + tm] * wc_ref[2:3, :])
    yc = pm[:, 3 * GROUP:4 * GROUP] * conv

    p = jnp.where(valid, proj[:, 6 * GROUP:7 * GROUP], 0.0)
    a2 = p + pltpu.roll(p, 1, 0)
    a4 = a2 + pltpu.roll(a2, 2, 0)
    a8 = a4 + pltpu.roll(a4, 4, 0)
    a16 = a8 + pltpu.roll(a8, 8, 0)
    w2 = a2[HALO:HALO + tm]
    w4 = pltpu.roll(a4, ne - 1, 0)[HALO:HALO + tm]
    w8 = pltpu.roll(a8, ne - 3, 0)[HALO:HALO + tm]
    w16 = pltpu.roll(a16, ne - 7, 0)[HALO:HALO + tm]
    grp = lax.broadcasted_iota(jnp.int32, (tm, GROUP), 1) // HEAD_DIM
    wsum = jnp.where(grp == 0, w2, jnp.where(grp == 1, w4, jnp.where(grp == 2, w8, w16)))
    left = jnp.left_shift(1, grp)
    t = lax.broadcasted_iota(jnp.int32, (tm, GROUP), 0) + i * tm
    cnt = jnp.minimum(t + left, seq) - jnp.maximum(t - left, 0)
    pooled = wsum / cnt.astype(F32) - p[HALO:HALO + tm]
    yd = _dot(pooled.astype(BF16), wp_ref[...]) * ps_ref[...]

    gg = gg_ref[...]
    ycat = jnp.concatenate([
        (_rms(yb) * gg[:, GROUP:2 * GROUP]).astype(BF16),
        (_rms(yc) * gg[:, 2 * GROUP:3 * GROUP]).astype(BF16),
        (_rms(yd) * gg[:, 3 * GROUP:4 * GROUP]).astype(BF16)], axis=1)
    part_ref[0] = _dot(ycat, wo_ref[...])


def _mix_in(x, mod, g1, w_in, d64, ws, bs, wc, wp, ps, gg, wo_bcd):
    b, seq, d = x.shape
    tm = TM_MIX
    nt = seq // tm
    hb = tm // HALO
    full = lambda *shape: pl.BlockSpec(shape, lambda bi, i: (0,) * len(shape))
    return pl.pallas_call(
        functools.partial(_mix_in_kernel, tm=tm, seq=seq),
        grid=(b, nt),
        in_specs=[
            pl.BlockSpec((1, tm, d), lambda bi, i: (bi, i, 0)),
            pl.BlockSpec((1, HALO, d), lambda bi, i: (bi, jnp.maximum(i * hb - 1, 0), 0)),
            pl.BlockSpec((1, HALO, d), lambda bi, i: (bi, jnp.minimum((i + 1) * hb, seq // HALO - 1), 0)),
            pl.BlockSpec((1, N_MOD, d), lambda bi, i: (bi, 0, 0)),
            full(1, d), full(d, D_IN), full(GROUP, 2 * GROUP), full(N_HEADS * CHUNK, CHUNK),
            full(CHUNK, GROUP), full(3, GROUP), full(GROUP, GROUP), full(1, GROUP),
            full(1, d), full(3 * GROUP, d),
        ],
        out_specs=[
            pl.BlockSpec((1, tm, 2 * GROUP), lambda bi, i: (bi, i, 0)),
            pl.BlockSpec((1, tm, d), lambda bi, i: (bi, i, 0)),
        ],
        out_shape=[
            jax.ShapeDtypeStruct((b, seq, 2 * GROUP), BF16),
            jax.ShapeDtypeStruct((b, seq, d), F32),
        ],
        compiler_params=pltpu.CompilerParams(
            dimension_semantics=("arbitrary", "arbitrary"), vmem_limit_bytes=VMEM_LIMIT),
        name="mix_in",
    )(x, x, x, mod, g1, w_in, d64, ws, bs, wc, wp, ps, gg, wo_bcd)


def _top2_of4(vals, aux):
    best, bi, ba = vals[0], jnp.zeros_like(vals[0], jnp.int32), aux[0]
    for j in range(1, 4):
        gt = vals[j] > best
        best = jnp.where(gt, vals[j], best)
        bi = jnp.where(gt, j, bi)
        ba = jnp.where(gt, aux[j], ba)
    sec = jnp.full_like(best, -jnp.inf)
    si, sa = jnp.zeros_like(bi), aux[0]
    for j in range(4):
        cand = jnp.where(bi == j, -jnp.inf, vals[j])
        gt = cand > sec
        sec = jnp.where(gt, cand, sec)
        si = jnp.where(gt, j, si)
        sa = jnp.where(gt, aux[j], sa)
    return best, sec, bi, si, ba, sa


def _mix_out_kernel(cs_ref, ss_ref, xg_ref, part_ref, x_ref, mod_ref, wf_ref, gg_ref, wo_ref,
                    g2_ref, wrh_ref, wrl_ref, br_ref, tri_ref, xo_ref, h2_ref, rt_ref, cnt_ref,
                    carry_ref):
    @pl.when((pl.program_id(0) == 0) & (pl.program_id(1) == 0))
    def _():
        carry_ref[...] = jnp.zeros_like(carry_ref)

    xg = xg_ref[0]
    f = _dot(cs_ref[...], xg[:, 0:GROUP]) + _dot(ss_ref[...], xg[:, GROUP:2 * GROUP])
    ya = _dot(f.astype(BF16), wf_ref[...])
    ya = _rms(ya) * gg_ref[:, 0:GROUP]
    mix = _dot(ya.astype(BF16), wo_ref[...]) + part_ref[0]
    mod = mod_ref[0]
    xn = x_ref[0] + mod[2:3] * mix
    xo_ref[0] = xn
    h2 = _rms(xn) * g2_ref[...] * (1.0 + mod[4:5]) + mod[3:4]
    h_hi = h2.astype(BF16)
    h2_ref[0] = h_hi
    h_lo = (h2 - h_hi.astype(F32)).astype(BF16)

    nt_dims = (((1,), (1,)), ((), ()))
    w_hi, w_lo = wrh_ref[...], wrl_ref[...]
    lt = (lax.dot_general(w_hi, h_hi, nt_dims, preferred_element_type=F32)
          + lax.dot_general(w_hi, h_lo, nt_dims, preferred_element_type=F32)
          + lax.dot_general(w_lo, h_hi, nt_dims, preferred_element_type=F32))
    ex = jnp.exp(lt - jnp.max(lt, axis=0, keepdims=True))
    scores = ex / jnp.sum(ex, axis=0, keepdims=True)
    biased = scores + br_ref[...]

    best_gs = None
    for g in range(N_EGROUPS):
        rows = [biased[g * 4 + j:g * 4 + j + 1] for j in range(4)]
        srow = [scores[g * 4 + j:g * 4 + j + 1] for j in range(4)]
        t1, t2, i1, i2, s1, s2 = _top2_of4(rows, srow)
        gs = t1 + t2
        if best_gs is None:
            best_gs, sel = gs, jnp.zeros_like(i1)
            bi1, bi2, bs1, bs2 = i1, i2, s1, s2
        else:
            gt = gs > best_gs
            best_gs = jnp.where(gt, gs, best_gs)
            sel = jnp.where(gt, g, sel)
            bi1, bi2 = jnp.where(gt, i1, bi1), jnp.where(gt, i2, bi2)
            bs1, bs2 = jnp.where(gt, s1, bs1), jnp.where(gt, s2, bs2)
    first_lo = bi1 < bi2
    lo = jnp.where(first_lo, bi1, bi2)
    hi = jnp.where(first_lo, bi2, bi1)
    den = bs1 + bs2
    w_a = jnp.where(first_lo, bs1, bs2) / den
    w_b = jnp.where(first_lo, bs2, bs1) / den
    pair = jnp.where(lo == 0, hi - 1, jnp.where(lo == 1, hi + 1, 5))
    cls_i = sel * len(PAIRS) + pair
    tk = cls_i.shape[1]

    onehot = lax.broadcasted_iota(jnp.int32, (CLS_PAD, tk), 0) == cls_i
    prefix = _dot(jnp.where(onehot, 1.0, 0.0).astype(BF16), tri_ref[...])
    carry = carry_ref[...]
    rank = jnp.sum(jnp.where(onehot, prefix - 1.0 + carry[:, 0:1], 0.0), axis=0, keepdims=True)
    carry = carry + prefix[:, tk - 1:tk]
    carry_ref[...] = carry
    cnt_ref[...] = carry
    rt_ref[0] = jnp.concatenate([cls_i.astype(F32), w_a, w_b, rank, jnp.zeros((4, tk), F32)], axis=0)


def _mix_out(cs, ss, xg, part, x, mod, wf_bd, gg, wo_a, g2, wr_hi, wr_lo, b_router, tri):
    b, seq, d = x.shape
    tk = TM_MIX
    nk = seq // tk
    full = lambda *shape: pl.BlockSpec(shape, lambda k, bi: (0,) * len(shape))
    return pl.pallas_call(
        _mix_out_kernel,
        grid=(nk, b),
        in_specs=[
            pl.BlockSpec((tk, seq), lambda k, bi: (k, 0)),
            pl.BlockSpec((tk, seq), lambda k, bi: (k, 0)),
            pl.BlockSpec((1, seq, 2 * GROUP), lambda k, bi: (bi, 0, 0)),
            pl.BlockSpec((1, tk, d), lambda k, bi: (bi, k, 0)),
            pl.BlockSpec((1, tk, d), lambda k, bi: (bi, k, 0)),
            pl.BlockSpec((1, N_MOD, d), lambda k, bi: (bi, 0, 0)),
            full(GROUP, GROUP), full(1, d), full(GROUP, d), full(1, d),
            full(N_EXPERTS, d), full(N_EXPERTS, d), full(N_EXPERTS, 1), full(tk, tk),
        ],
        out_specs=[
            pl.BlockSpec((1, tk, d), lambda k, bi: (bi, k, 0)),
            pl.BlockSpec((1, tk, d), lambda k, bi: (bi, k, 0)),
            pl.BlockSpec((1, 8, tk), lambda k, bi: (bi, 0, k)),
            full(CLS_PAD, 128),
        ],
        out_shape=[
            jax.ShapeDtypeStruct((b, seq, d), F32),
            jax.ShapeDtypeStruct((b, seq, d), BF16),
            jax.ShapeDtypeStruct((b, 8, seq), F32),
            jax.ShapeDtypeStruct((CLS_PAD, 128), F32),
        ],
        scratch_shapes=[pltpu.VMEM((CLS_PAD, 128), F32)],
        compiler_params=pltpu.CompilerParams(
            dimension_semantics=("arbitrary", "arbitrary"), vmem_limit_bytes=VMEM_LIMIT),
        name="mix_out",
    )(cs, ss, xg, part, x, mod, wf_bd, gg, wo_a, g2, wr_hi, wr_lo, b_router, tri)


def _moe_kernel(ea_ref, eb_ref, nu_ref, xs_ref, wt_ref, wga_ref, wua_ref, wda_ref,
                wgb_ref, wub_ref, wdb_ref, o_ref):
    @pl.when(pl.program_id(0) < nu_ref[0])
    def _():
        xs = xs_ref[...]
        wt = wt_ref[...]

        def ffn(wg, wu, wd):
            a = _dot(xs, wg[0])
            hid = (a * jax.nn.sigmoid(a)) * _dot(xs, wu[0])
            return _dot(hid.astype(BF16), wd[0])

        o_ref[...] = (wt[:, 0:1] * ffn(wga_ref, wua_ref, wda_ref)
                      + wt[:, 1:2] * ffn(wgb_ref, wub_ref, wdb_ref)).astype(o_ref.dtype)


def _moe_ffn(tile_ea, tile_eb, n_used, xs, wts, wg, wu, wd):
    p, d = xs.shape
    tm = TM_MOE
    nt = p // tm
    row = lambda i, ea, eb, nu: (jnp.minimum(i, nu[0] - 1), 0)
    exp_a = lambda i, ea, eb, nu: (ea[i], 0, 0)
    exp_b = lambda i, ea, eb, nu: (eb[i], 0, 0)
    return pl.pallas_call(
        _moe_kernel,
        grid_spec=pltpu.PrefetchScalarGridSpec(
            num_scalar_prefetch=3,
            grid=(nt,),
            in_specs=[
                pl.BlockSpec((tm, d), row),
                pl.BlockSpec((tm, 2), row),
                pl.BlockSpec((1, d, D_EXPERT), exp_a),
                pl.BlockSpec((1, d, D_EXPERT), exp_a),
                pl.BlockSpec((1, D_EXPERT, d), exp_a),
                pl.BlockSpec((1, d, D_EXPERT), exp_b),
                pl.BlockSpec((1, d, D_EXPERT), exp_b),
                pl.BlockSpec((1, D_EXPERT, d), exp_b),
            ],
            out_specs=pl.BlockSpec((tm, d), row),
        ),
        out_shape=jax.ShapeDtypeStruct((p, d), BF16),
        compiler_params=pltpu.CompilerParams(
            dimension_semantics=("arbitrary",), vmem_limit_bytes=VMEM_LIMIT),
        name="moe_ffn",
    )(tile_ea, tile_eb, n_used, xs, wts, wg, wu, wd, wg, wu, wd)


def _combine_kernel(x_ref, y_ref, mod_ref, gf_ref, o_ref, *, final):
    xn = x_ref[0] + mod_ref[0][5:6] * y_ref[0].astype(F32)
    if final:
        xn = _rms(xn) * gf_ref[...]
    o_ref[0] = xn


def _combine(x, y, mod, g_final, final):
    b, seq, d = x.shape
    tm = TM_MIX
    blk = pl.BlockSpec((1, tm, d), lambda bi, i: (bi, i, 0))
    return pl.pallas_call(
        functools.partial(_combine_kernel, final=final),
        grid=(b, seq // tm),
        in_specs=[blk, blk, pl.BlockSpec((1, N_MOD, d), lambda bi, i: (bi, 0, 0)),
                  pl.BlockSpec((1, d), lambda bi, i: (0, 0))],
        out_specs=blk,
        out_shape=jax.ShapeDtypeStruct((b, seq, d), F32),
        compiler_params=pltpu.CompilerParams(
            dimension_semantics=("arbitrary", "arbitrary"), vmem_limit_bytes=VMEM_LIMIT),
        name="combine",
    )(x, y, mod, g_final)


def _dft_tables(seq):
    n = np.arange(seq)
    ang = 2.0 * np.pi * ((n[:, None] * n[None, :]) % seq) / seq
    scale = 1.0 / np.sqrt(seq)
    return (np.cos(ang) * scale).astype(np.float32), (-np.sin(ang) * scale).astype(np.float32)


def _dft64_blockdiag():
    n = np.arange(HEAD_DIM)
    ang = 2.0 * np.pi * ((n[:, None] * n[None, :]) % HEAD_DIM) / HEAD_DIM
    c, s = np.cos(ang) / np.sqrt(HEAD_DIM), np.sin(ang) / np.sqrt(HEAD_DIM)
    bd = np.zeros((GROUP, 2 * GROUP), np.float32)
    for h in range(N_HEADS):
        r = slice(h * HEAD_DIM, (h + 1) * HEAD_DIM)
        bd[r, h * HEAD_DIM:(h + 1) * HEAD_DIM] = c
        bd[r, GROUP + h * HEAD_DIM:GROUP + (h + 1) * HEAD_DIM] = s
    return bd


def _block_diag(w):
    n, k, _ = w.shape
    eye = jnp.eye(n, dtype=w.dtype)
    return (eye[:, None, :, None] * w[:, :, None, :]).reshape(n * k, n * k)


def _route_tables(cls, rank, counts, tm):
    t = cls.shape[0]
    nt = t // tm + N_CLASSES
    ntile_c = (counts + tm - 1) // tm
    tile_end = jnp.cumsum(ntile_c)
    pstart = (tile_end - ntile_c) * tm
    inv = jnp.take(pstart, cls) + rank
    n_used = tile_end[-1]
    tile_ids = jnp.minimum(jnp.arange(nt, dtype=jnp.int32), n_used - 1)
    tile_cls = jnp.sum((tile_ids[:, None] >= tile_end[None, :]).astype(jnp.int32), axis=1)
    pair_lo = jnp.array([p[0] for p in PAIRS], jnp.int32)
    pair_hi = jnp.array([p[1] for p in PAIRS], jnp.int32)
    grp, pid = tile_cls // len(PAIRS), tile_cls % len(PAIRS)
    tile_ea = grp * EXPERTS_PER_GROUP + jnp.take(pair_lo, pid)
    tile_eb = grp * EXPERTS_PER_GROUP + jnp.take(pair_hi, pid)
    return inv, tile_ea, tile_eb, n_used.reshape(1).astype(jnp.int32), nt


def _trunk(x, mods, lw, shared):
    b, seq, d = x.shape
    t = b * seq
    depth = len(lw)
    for l in range(depth):
        w = lw[l]
        mod = mods[l]
        xg, part = _mix_in(x, mod, w["g1"], w["w_in"], shared["d64"], w["ws"], w["bs"], w["wc"],
                           w["wp"], w["ps"], w["gg"], w["wo_bcd"])
        x_mid, h2, route, cnt = _mix_out(shared["cs"], shared["ss"], xg, part, x, mod, w["wf"],
                                         w["gg"], w["wo_a"], w["g2"], shared["wr_hi"],
                                         shared["wr_lo"], shared["b_router"], shared["tri"])
        cls = route[:, 0, :].reshape(t).astype(jnp.int32)
        rank = route[:, 3, :].reshape(t).astype(jnp.int32)
        counts = cnt[:N_CLASSES, 0].astype(jnp.int32)
        wts = jnp.stack([route[:, 1, :].reshape(t), route[:, 2, :].reshape(t)], axis=1)
        wts = jnp.concatenate([wts, jnp.zeros((1, 2), F32)], axis=0)
        inv, tile_ea, tile_eb, n_used, nt = _route_tables(cls, rank, counts, TM_MOE)
        p = nt * TM_MOE
        src = jnp.full((p,), t, jnp.int32).at[inv].set(jnp.arange(t, dtype=jnp.int32))
        wts_s = jnp.take(wts, src, axis=0)
        xs = jnp.take(h2.reshape(t, d), jnp.minimum(src, t - 1), axis=0)
        ys = _moe_ffn(tile_ea, tile_eb, n_used, xs, wts_s, w["wg"], w["wu"], w["wd"])
        y_tok = jnp.take(ys, inv, axis=0).reshape(b, seq, d)
        x = _combine(x_mid, y_tok, mod, shared["g_final"], final=(l == depth - 1))
    return x


def kernel(x_prompt, x_sample, c_prompt, c_sample, w_ada, b_ada, g_norm1, w_in, w_fourier, w_spatial, b_spatial, w_conv, w_pool, pool_scale, g_group, w_out, g_norm2, w_router, b_router, w_exp_gate, w_exp_up, w_exp_down, g_final):
    depth = w_in.shape[0]
    seq = x_prompt.shape[1]
    d = D_MODEL
    nb_p = c_prompt.shape[0]

    cs, ss = _dft_tables(seq)
    wr_hi = w_router.astype(BF16)
    wr_lo = (w_router - wr_hi.astype(F32)).astype(BF16)
    shared = {
        "cs": jnp.asarray(cs).astype(BF16),
        "ss": jnp.asarray(ss).astype(BF16),
        "d64": jnp.asarray(_dft64_blockdiag()).astype(BF16),
        "wr_hi": wr_hi.T, "wr_lo": wr_lo.T,
        "b_router": b_router.reshape(N_EXPERTS, 1).astype(F32),
        "g_final": g_final.reshape(1, d),
        "tri": jnp.asarray(np.triu(np.ones((TM_MIX, TM_MIX), np.float32))).astype(BF16),
    }
    lw = []
    for l in range(depth):
        lw.append({
            "g1": g_norm1[l].reshape(1, d),
            "w_in": w_in[l].astype(BF16),
            "ws": w_spatial[l].reshape(N_HEADS * CHUNK, CHUNK).astype(BF16),
            "bs": jnp.repeat(b_spatial[l].T, HEAD_DIM, axis=1),
            "wc": w_conv[l],
            "wp": _block_diag(w_pool[l]).astype(BF16),
            "ps": pool_scale[l].reshape(1, GROUP),
            "gg": g_group[l].reshape(1, d),
            "wo_bcd": w_out[l, GROUP:].astype(BF16),
            "wo_a": w_out[l, :GROUP].astype(BF16),
            "wf": _block_diag(w_fourier[l]).astype(BF16),
            "g2": g_norm2[l].reshape(1, d),
            "wg": w_exp_gate[l].astype(BF16),
            "wu": w_exp_up[l].astype(BF16),
            "wd": w_exp_down[l].astype(BF16),
        })

    c_all = jnp.concatenate([c_prompt, c_sample], axis=0)
    mod_all = _ada(c_all, w_ada, b_ada).reshape(depth, c_all.shape[0], N_MOD, d)
    mods_p = [mod_all[l, :nb_p] for l in range(depth)]
    mods_s = [mod_all[l, nb_p:] for l in range(depth)]
    y_prompt = _trunk(x_prompt, mods_p, lw, shared)
    y_sample = _trunk(x_sample, mods_s, lw, shared)
    return (y_prompt, y_sample)
```

```python
import functools

import numpy as np
import jax
import jax.numpy as jnp
from jax import lax
from jax.experimental import pallas as pl
from jax.experimental.pallas import tpu as pltpu
from jax.experimental.pallas import tpu_sc as plsc

F32 = jnp.float32
BF16 = jnp.bfloat16
I32 = jnp.int32
U32 = jnp.uint32

D_MODEL = 1024
GROUP = 256
N_HEADS = 4
HEAD_DIM = 64
CHUNK = 128
D_IN = 7 * GROUP
N_EXPERTS = 16
N_EGROUPS = 4
EXPERTS_PER_GROUP = 4
D_EXPERT = 512
N_MOD = 6
EPS = 1e-6
HALO = 8
PAIRS = ((0, 1), (0, 2), (0, 3), (1, 2), (1, 3), (2, 3))
N_CLASSES = N_EGROUPS * len(PAIRS)
CLS_PAD = 32

TM_MIX = 512
TM_MOE = 512
VMEM_LIMIT = 56 * 1024 * 1024

W_PACK = D_MODEL // 2
W_EXTRA = 128
W_ROW = W_PACK + W_EXTRA

SC_CORES = 2
SC_SUBCORES = 16
SC_LANES = 16
SC_WORKERS = SC_CORES * SC_SUBCORES
SC_ROWS = 64
SC_SCAN = 2048


def _rms(x):
    return x * lax.rsqrt(jnp.mean(x * x, axis=-1, keepdims=True) + EPS)


def _dot(a, b):
    return jnp.dot(a, b, preferred_element_type=F32)


def _pack_pairs(x):
    k = x.shape[1] // 2
    xb = x.astype(BF16).astype(F32)
    hi = lax.bitcast_convert_type(xb[:, :k], U32)
    lo = lax.bitcast_convert_type(xb[:, k:], U32)
    return lax.bitcast_convert_type(hi | (lo >> 16), I32)


def _unpack_pairs(w):
    u = lax.bitcast_convert_type(w, U32)
    hi = lax.bitcast_convert_type(u & jnp.uint32(0xFFFF0000), F32)
    lo = lax.bitcast_convert_type(u << 16, F32)
    return jnp.concatenate([hi, lo], axis=1)


def _ada_kernel(c_ref, w_ref, b_ref, o_ref):
    c = c_ref[...]
    sc = c * jax.nn.sigmoid(c)
    o_ref[0] = _dot(sc.astype(BF16), w_ref[0].astype(BF16)) + b_ref[0]


def _ada(c, w_ada, b_ada):
    depth, d, n = w_ada.shape
    bc = c.shape[0]
    tn = 1536
    return pl.pallas_call(
        _ada_kernel,
        grid=(depth, n // tn),
        in_specs=[
            pl.BlockSpec((bc, d), lambda l, j: (0, 0)),
            pl.BlockSpec((1, d, tn), lambda l, j: (l, 0, j)),
            pl.BlockSpec((1, 1, tn), lambda l, j: (l, 0, j)),
        ],
        out_specs=pl.BlockSpec((1, bc, tn), lambda l, j: (l, 0, j)),
        out_shape=jax.ShapeDtypeStruct((depth, bc, n), F32),
        compiler_params=pltpu.CompilerParams(
            dimension_semantics=("arbitrary", "arbitrary"), vmem_limit_bytes=VMEM_LIMIT),
        name="ada_mod",
    )(c, w_ada, b_ada.reshape(depth, 1, n))


def _mix_in_kernel(xm_ref, xp_ref, xn_ref, mod_ref, g1_ref, win_ref, d64_ref, ws_ref, bs_ref,
                   wc_ref, wp_ref, ps_ref, gg_ref, wo_ref, xg_ref, part_ref, *, tm, seq):
    i = pl.program_id(1)
    ne = tm + 2 * HALO
    xe = jnp.concatenate([xp_ref[0], xm_ref[0], xn_ref[0]], axis=0)
    mod = mod_ref[0]
    h = _rms(xe) * g1_ref[...] * (1.0 + mod[1:2]) + mod[0:1]
    proj = _dot(h.astype(BF16), win_ref[...])

    gpos = lax.broadcasted_iota(jnp.int32, (ne, GROUP), 0) + (i * tm - HALO)
    valid = (gpos >= 0) & (gpos < seq)
    pm = proj[HALO:HALO + tm]

    xg_ref[0] = _dot(pm[:, 0:GROUP].astype(BF16), d64_ref[...]).astype(BF16)

    u = pm[:, GROUP:2 * GROUP]
    v = pm[:, 2 * GROUP:3 * GROUP]
    head = lax.broadcasted_iota(jnp.int32, (CHUNK, GROUP), 1) // HEAD_DIM
    yb_chunks = []
    for c in range(tm // CHUNK):
        vc = v[c * CHUNK:(c + 1) * CHUNK].astype(BF16)
        m_all = _dot(ws_ref[...], vc)
        mixed = bs_ref[...]
        for hh in range(N_HEADS):
            mixed = mixed + jnp.where(head == hh, m_all[hh * CHUNK:(hh + 1) * CHUNK], 0.0)
        yb_chunks.append(u[c * CHUNK:(c + 1) * CHUNK] * mixed)
    yb = jnp.concatenate(yb_chunks, axis=0)

    z = jnp.where(valid, proj[:, 4 * GROUP:5 * GROUP] * proj[:, 5 * GROUP:6 * GROUP], 0.0)
    conv = (pltpu.roll(z, 1, 0)[HALO:HALO + tm] * wc_ref[0:1, :]
            + z[HALO:HALO + tm] * wc_ref[1:2, :]
            + pltpu.roll(z, ne - 1, 0)[HALO:HALO + tm] * wc_ref[2:3, :])
    yc = pm[:, 3 * GROUP:4 * GROUP] * conv

    p = jnp.where(valid, proj[:, 6 * GROUP:7 * GROUP], 0.0)
    a2 = p + pltpu.roll(p, 1, 0)
    a4 = a2 + pltpu.roll(a2, 2, 0)
    a8 = a4 + pltpu.roll(a4, 4, 0)
    a16 = a8 + pltpu.roll(a8, 8, 0)
    w2 = a2[HALO:HALO + tm]
    w4 = pltpu.roll(a4, ne - 1, 0)[HALO:HALO + tm]
    w8 = pltpu.roll(a8, ne - 3, 0)[HALO:HALO + tm]
    w16 = pltpu.roll(a16, ne - 7, 0)[HALO:HALO + tm]
    grp = lax.broadcasted_iota(jnp.int32, (tm, GROUP), 1) // HEAD_DIM
    wsum = jnp.where(grp == 0, w2, jnp.where(grp == 1, w4, jnp.where(grp == 2, w8, w16)))
    left = jnp.left_shift(1, grp)
    t = lax.broadcasted_iota(jnp.int32, (tm, GROUP), 0) + i * tm
    cnt = jnp.minimum(t + left, seq) - jnp.maximum(t - left, 0)
    pooled = wsum / cnt.astype(F32) - p[HALO:HALO + tm]
    yd = _dot(pooled.astype(BF16), wp_ref[...]) * ps_ref[...]

    gg = gg_ref[...]
    ycat = jnp.concatenate([
        (_rms(yb) * gg[:, GROUP:2 * GROUP]).astype(BF16),
        (_rms(yc) * gg[:, 2 * GROUP:3 * GROUP]).astype(BF16),
        (_rms(yd) * gg[:, 3 * GROUP:4 * GROUP]).astype(BF16)], axis=1)
    part_ref[0] = _dot(ycat, wo_ref[...])


def _mix_in(x, mod, g1, w_in, d64, ws, bs, wc, wp, ps, gg, wo_bcd):
    b, seq, d = x.shape
    tm = TM_MIX
    nt = seq // tm
    hb = tm // HALO
    full = lambda *shape: pl.BlockSpec(shape, lambda bi, i: (0,) * len(shape))
    return pl.pallas_call(
        functools.partial(_mix_in_kernel, tm=tm, seq=seq),
        grid=(b, nt),
        in_specs=[
            pl.BlockSpec((1, tm, d), lambda bi, i: (bi, i, 0)),
            pl.BlockSpec((1, HALO, d), lambda bi, i: (bi, jnp.maximum(i * hb - 1, 0), 0)),
            pl.BlockSpec((1, HALO, d), lambda bi, i: (bi, jnp.minimum((i + 1) * hb, seq // HALO - 1), 0)),
            pl.BlockSpec((1, N_MOD, d), lambda bi, i: (bi, 0, 0)),
            full(1, d), full(d, D_IN), full(GROUP, 2 * GROUP), full(N_HEADS * CHUNK, CHUNK),
            full(CHUNK, GROUP), full(3, GROUP), full(GROUP, GROUP), full(1, GROUP),
            full(1, d), full(3 * GROUP, d),
        ],
        out_specs=[
            pl.BlockSpec((1, tm, 2 * GROUP), lambda bi, i: (bi, i, 0)),
            pl.BlockSpec((1, tm, d), lambda bi, i: (bi, i, 0)),
        ],
        out_shape=[
            jax.ShapeDtypeStruct((b, seq, 2 * GROUP), BF16),
            jax.ShapeDtypeStruct((b, seq, d), F32),
        ],
        compiler_params=pltpu.CompilerParams(
            dimension_semantics=("arbitrary", "arbitrary"), vmem_limit_bytes=VMEM_LIMIT),
        name="mix_in",
    )(x, x, x, mod, g1, w_in, d64, ws, bs, wc, wp, ps, gg, wo_bcd)


def _top2_of4(vals, aux):
    best, bi, ba = vals[0], jnp.zeros_like(vals[0], jnp.int32), aux[0]
    for j in range(1, 4):
        gt = vals[j] > best
        best = jnp.where(gt, vals[j], best)
        bi = jnp.where(gt, j, bi)
        ba = jnp.where(gt, aux[j], ba)
    sec = jnp.full_like(best, -jnp.inf)
    si, sa = jnp.zeros_like(bi), aux[0]
    for j in range(4):
        cand = jnp.where(bi == j, -jnp.inf, vals[j])
        gt = cand > sec
        sec = jnp.where(gt, cand, sec)
        si = jnp.where(gt, j, si)
        sa = jnp.where(gt, aux[j], sa)
    return best, sec, bi, si, ba, sa


def _mix_out_kernel(cs_ref, ss_ref, xg_ref, part_ref, x_ref, mod_ref, wf_ref, gg_ref, wo_ref,
                    g2_ref, wrh_ref, wrl_ref, br_ref, tri_ref, xo_ref, h2_ref, rt_ref, cnt_ref,
                    carry_ref):
    @pl.when((pl.program_id(0) == 0) & (pl.program_id(1) == 0))
    def _():
        carry_ref[...] = jnp.zeros_like(carry_ref)

    xg = xg_ref[0]
    f = _dot(cs_ref[...], xg[:, 0:GROUP]) + _dot(ss_ref[...], xg[:, GROUP:2 * GROUP])
    ya = _dot(f.astype(BF16), wf_ref[...])
    ya = _rms(ya) * gg_ref[:, 0:GROUP]
    mix = _dot(ya.astype(BF16), wo_ref[...]) + part_ref[0]
    mod = mod_ref[0]
    xn = x_ref[0] + mod[2:3] * mix
    xo_ref[0] = xn
    h2 = _rms(xn) * g2_ref[...] * (1.0 + mod[4:5]) + mod[3:4]
    h_hi = h2.astype(BF16)
    h2_ref[0, :, 0:W_PACK] = _pack_pairs(h2)
    h_lo = (h2 - h_hi.astype(F32)).astype(BF16)

    nt_dims = (((1,), (1,)), ((), ()))
    w_hi, w_lo = wrh_ref[...], wrl_ref[...]
    lt = (lax.dot_general(w_hi, h_hi, nt_dims, preferred_element_type=F32)
          + lax.dot_general(w_hi, h_lo, nt_dims, preferred_element_type=F32)
          + lax.dot_general(w_lo, h_hi, nt_dims, preferred_element_type=F32))
    ex = jnp.exp(lt - jnp.max(lt, axis=0, keepdims=True))
    scores = ex / jnp.sum(ex, axis=0, keepdims=True)
    biased = scores + br_ref[...]

    best_gs = None
    for g in range(N_EGROUPS):
        rows = [biased[g * 4 + j:g * 4 + j + 1] for j in range(4)]
        srow = [scores[g * 4 + j:g * 4 + j + 1] for j in range(4)]
        t1, t2, i1, i2, s1, s2 = _top2_of4(rows, srow)
        gs = t1 + t2
        if best_gs is None:
            best_gs, sel = gs, jnp.zeros_like(i1)
            bi1, bi2, bs1, bs2 = i1, i2, s1, s2
        else:
            gt = gs > best_gs
            best_gs = jnp.where(gt, gs, best_gs)
            sel = jnp.where(gt, g, sel)
            bi1, bi2 = jnp.where(gt, i1, bi1), jnp.where(gt, i2, bi2)
            bs1, bs2 = jnp.where(gt, s1, bs1), jnp.where(gt, s2, bs2)
    first_lo = bi1 < bi2
    lo = jnp.where(first_lo, bi1, bi2)
    hi = jnp.where(first_lo, bi2, bi1)
    den = bs1 + bs2
    w_a = jnp.where(first_lo, bs1, bs2) / den
    w_b = jnp.where(first_lo, bs2, bs1) / den
    pair = jnp.where(lo == 0, hi - 1, jnp.where(lo == 1, hi + 1, 5))
    cls_i = sel * len(PAIRS) + pair
    tk = cls_i.shape[1]
    gate_cols = jnp.concatenate([w_a, w_b, jnp.zeros((W_EXTRA - 2, tk), F32)], axis=0).T
    h2_ref[0, :, W_PACK:W_ROW] = lax.bitcast_convert_type(gate_cols, I32)

    onehot = lax.broadcasted_iota(jnp.int32, (CLS_PAD, tk), 0) == cls_i
    prefix = _dot(jnp.where(onehot, 1.0, 0.0).astype(BF16), tri_ref[...])
    carry = carry_ref[...]
    rank = jnp.sum(jnp.where(onehot, prefix - 1.0 + carry[:, 0:1], 0.0), axis=0, keepdims=True)
    carry = carry + prefix[:, tk - 1:tk]
    carry_ref[...] = carry
    cnt_ref[...] = carry
    rt_ref[0] = jnp.concatenate(
        [cls_i, rank.astype(I32), jnp.zeros((6, tk), I32)], axis=0)


def _mix_out(cs, ss, xg, part, x, mod, wf_bd, gg, wo_a, g2, wr_hi, wr_lo, b_router, tri):
    b, seq, d = x.shape
    tk = TM_MIX
    nk = seq // tk
    full = lambda *shape: pl.BlockSpec(shape, lambda k, bi: (0,) * len(shape))
    return pl.pallas_call(
        _mix_out_kernel,
        grid=(nk, b),
        in_specs=[
            pl.BlockSpec((tk, seq), lambda k, bi: (k, 0)),
            pl.BlockSpec((tk, seq), lambda k, bi: (k, 0)),
            pl.BlockSpec((1, seq, 2 * GROUP), lambda k, bi: (bi, 0, 0)),
            pl.BlockSpec((1, tk, d), lambda k, bi: (bi, k, 0)),
            pl.BlockSpec((1, tk, d), lambda k, bi: (bi, k, 0)),
            pl.BlockSpec((1, N_MOD, d), lambda k, bi: (bi, 0, 0)),
            full(GROUP, GROUP), full(1, d), full(GROUP, d), full(1, d),
            full(N_EXPERTS, d), full(N_EXPERTS, d), full(N_EXPERTS, 1), full(tk, tk),
        ],
        out_specs=[
            pl.BlockSpec((1, tk, d), lambda k, bi: (bi, k, 0)),
            pl.BlockSpec((1, tk, W_ROW), lambda k, bi: (bi, k, 0)),
            pl.BlockSpec((1, 8, tk), lambda k, bi: (bi, 0, k)),
            full(CLS_PAD, 128),
        ],
        out_shape=[
            jax.ShapeDtypeStruct((b, seq, d), F32),
            jax.ShapeDtypeStruct((b, seq, W_ROW), I32),
            jax.ShapeDtypeStruct((b, 8, seq), I32),
            jax.ShapeDtypeStruct((CLS_PAD, 128), F32),
        ],
        scratch_shapes=[pltpu.VMEM((CLS_PAD, 128), F32)],
        compiler_params=pltpu.CompilerParams(
            dimension_semantics=("arbitrary", "arbitrary"), vmem_limit_bytes=VMEM_LIMIT),
        name="mix_out",
    )(cs, ss, xg, part, x, mod, wf_bd, gg, wo_a, g2, wr_hi, wr_lo, b_router, tri)


def _moe_kernel(ea_ref, eb_ref, nu_ref, xs_ref, wga_ref, wua_ref, wda_ref,
                wgb_ref, wub_ref, wdb_ref, o_ref):
    @pl.when(pl.program_id(0) < nu_ref[0])
    def _():
        xs = _unpack_pairs(xs_ref[:, 0:W_PACK]).astype(BF16)
        wt = lax.bitcast_convert_type(xs_ref[:, W_PACK:W_ROW], F32)

        def ffn(wg, wu, wd):
            a = _dot(xs, wg[0])
            hid = (a * jax.nn.sigmoid(a)) * _dot(xs, wu[0])
            return _dot(hid.astype(BF16), wd[0])

        o_ref[...] = _pack_pairs(wt[:, 0:1] * ffn(wga_ref, wua_ref, wda_ref)
                                 + wt[:, 1:2] * ffn(wgb_ref, wub_ref, wdb_ref))


def _moe_ffn(tile_ea, tile_eb, n_used, xs, wg, wu, wd):
    p = xs.shape[0]
    d = D_MODEL
    tm = TM_MOE
    nt = p // tm
    row = lambda i, ea, eb, nu: (jnp.minimum(i, nu[0] - 1), 0)
    exp_a = lambda i, ea, eb, nu: (ea[i], 0, 0)
    exp_b = lambda i, ea, eb, nu: (eb[i], 0, 0)
    return pl.pallas_call(
        _moe_kernel,
        grid_spec=pltpu.PrefetchScalarGridSpec(
            num_scalar_prefetch=3,
            grid=(nt,),
            in_specs=[
                pl.BlockSpec((tm, W_ROW), row),
                pl.BlockSpec((1, d, D_EXPERT), exp_a),
                pl.BlockSpec((1, d, D_EXPERT), exp_a),
                pl.BlockSpec((1, D_EXPERT, d), exp_a),
                pl.BlockSpec((1, d, D_EXPERT), exp_b),
                pl.BlockSpec((1, d, D_EXPERT), exp_b),
                pl.BlockSpec((1, D_EXPERT, d), exp_b),
            ],
            out_specs=pl.BlockSpec((tm, W_PACK), row),
        ),
        out_shape=jax.ShapeDtypeStruct((p, W_PACK), I32),
        compiler_params=pltpu.CompilerParams(
            dimension_semantics=("arbitrary",), vmem_limit_bytes=VMEM_LIMIT),
        name="moe_ffn",
    )(tile_ea, tile_eb, n_used, xs, wg, wu, wd, wg, wu, wd)


def _sc_worker_id():
    return lax.axis_index("s") * SC_CORES + lax.axis_index("c")


def _sc_gather_rows(table_hbm, out_hbm, idx_all, out_base, n_chunks, idx_bufs, row_bufs, sems):
    def copy(slot):
        return pltpu.make_async_copy(table_hbm.at[idx_bufs[slot]], row_bufs[slot], sems[slot])

    def start(j, slot):
        for q in range(SC_ROWS // SC_LANES):
            idx_bufs[slot][pl.ds(q * SC_LANES, SC_LANES)] = (
                idx_all[pl.ds(j * SC_ROWS + q * SC_LANES, SC_LANES)])
        copy(slot).start()

    def flush(j, slot):
        copy(slot).wait()
        pltpu.sync_copy(row_bufs[slot], out_hbm.at[pl.ds(out_base + j * SC_ROWS, SC_ROWS)])

    start(0, 0)

    @pl.loop(0, n_chunks // 2)
    def _(jj):
        j = 2 * jj
        start(j + 1, 1)
        flush(j, 0)

        @pl.when(j + 2 < n_chunks)
        def _():
            start(j + 2, 0)

        flush(j + 1, 1)


def _sc_scratch(n_idx, width):
    return [
        pltpu.VMEM((n_idx,), I32),
        pltpu.VMEM((SC_ROWS,), I32), pltpu.VMEM((SC_ROWS,), I32),
        pltpu.VMEM((SC_ROWS, width), I32), pltpu.VMEM((SC_ROWS, width), I32),
        pltpu.SemaphoreType.DMA, pltpu.SemaphoreType.DMA,
    ]


def _sc_dispatch(cls, rank, pstart, table, p):
    t, width = table.shape
    pw = p // SC_WORKERS
    n_scan = t // SC_SCAN
    n_chunks = pw // SC_ROWS
    assert p % (SC_WORKERS * 2 * SC_ROWS) == 0 and t % SC_SCAN == 0 and n_scan <= SC_WORKERS
    mesh = plsc.VectorSubcoreMesh(core_axis_name="c", subcore_axis_name="s")

    @functools.partial(
        pl.kernel, mesh=mesh,
        out_type=[jax.ShapeDtypeStruct((p, width), I32), jax.ShapeDtypeStruct((t,), I32)],
        scratch_types=[pltpu.VMEM((SC_SCAN,), I32), pltpu.VMEM((SC_SCAN,), I32),
                       pltpu.VMEM((SC_SCAN,), I32), pltpu.VMEM((CLS_PAD,), I32)]
        + _sc_scratch(pw, width),
        compiler_params=pltpu.CompilerParams(needs_layout_passes=False),
        name="dispatch")
    def k(cls_hbm, rank_hbm, ps_hbm, table_hbm, xs_hbm, inv_hbm,
          cls_v, rank_v, inv_v, ps_v, src_v, idx_a, idx_b, rows_a, rows_b, sem_a, sem_b):
        wid = _sc_worker_id()
        lo = wid * pw
        pltpu.sync_copy(ps_hbm, ps_v)

        @pl.loop(0, pw // SC_LANES)
        def _(i):
            src_v[pl.ds(i * SC_LANES, SC_LANES)] = jnp.zeros((SC_LANES,), I32)

        @pl.loop(0, n_scan)
        def _(c):
            pltpu.sync_copy(cls_hbm.at[pl.ds(c * SC_SCAN, SC_SCAN)], cls_v)
            pltpu.sync_copy(rank_hbm.at[pl.ds(c * SC_SCAN, SC_SCAN)], rank_v)

            @pl.loop(0, SC_SCAN // SC_LANES)
            def _(i):
                sl = pl.ds(i * SC_LANES, SC_LANES)
                pos = plsc.load_gather(ps_v, [cls_v[sl]]) + rank_v[sl]
                inv_v[sl] = pos
                tok = c * SC_SCAN + i * SC_LANES + lax.iota(I32, SC_LANES)
                mine = (pos >= lo) & (pos < lo + pw)
                plsc.store_scatter(src_v, [pos - lo], tok, mask=mine)

            @pl.when(c == wid)
            def _():
                pltpu.sync_copy(inv_v, inv_hbm.at[pl.ds(c * SC_SCAN, SC_SCAN)])

        _sc_gather_rows(table_hbm, xs_hbm, src_v, lo, n_chunks,
                        (idx_a, idx_b), (rows_a, rows_b), (sem_a, sem_b))

    return k(cls, rank, pstart, table)


def _sc_unsort(table, inv):
    t = inv.shape[0]
    width = table.shape[1]
    tw = t // SC_WORKERS
    n_chunks = tw // SC_ROWS
    assert t % (SC_WORKERS * 2 * SC_ROWS) == 0
    mesh = plsc.VectorSubcoreMesh(core_axis_name="c", subcore_axis_name="s")

    @functools.partial(
        pl.kernel, mesh=mesh, out_type=jax.ShapeDtypeStruct((t, width), I32),
        scratch_types=_sc_scratch(tw, width),
        compiler_params=pltpu.CompilerParams(needs_layout_passes=False),
        name="unsort")
    def k(table_hbm, inv_hbm, out_hbm, idx_all, idx_a, idx_b, rows_a, rows_b, sem_a, sem_b):
        base = _sc_worker_id() * tw
        pltpu.sync_copy(inv_hbm.at[pl.ds(base, tw)], idx_all)
        _sc_gather_rows(table_hbm, out_hbm, idx_all, base, n_chunks,
                        (idx_a, idx_b), (rows_a, rows_b), (sem_a, sem_b))

    return k(table, inv)


def _combine_kernel(x_ref, y_ref, mod_ref, gf_ref, o_ref, *, final):
    xn = x_ref[0] + mod_ref[0][5:6] * _unpack_pairs(y_ref[0])
    if final:
        xn = _rms(xn) * gf_ref[...]
    o_ref[0] = xn


def _combine(x, y, mod, g_final, final):
    b, seq, d = x.shape
    tm = TM_MIX
    blk = pl.BlockSpec((1, tm, d), lambda bi, i: (bi, i, 0))
    return pl.pallas_call(
        functools.partial(_combine_kernel, final=final),
        grid=(b, seq // tm),
        in_specs=[blk, pl.BlockSpec((1, tm, W_PACK), lambda bi, i: (bi, i, 0)),
                  pl.BlockSpec((1, N_MOD, d), lambda bi, i: (bi, 0, 0)),
                  pl.BlockSpec((1, d), lambda bi, i: (0, 0))],
        out_specs=blk,
        out_shape=jax.ShapeDtypeStruct((b, seq, d), F32),
        compiler_params=pltpu.CompilerParams(
            dimension_semantics=("arbitrary", "arbitrary"), vmem_limit_bytes=VMEM_LIMIT),
        name="combine",
    )(x, y, mod, g_final)


def _dft_tables(seq):
    n = np.arange(seq)
    ang = 2.0 * np.pi * ((n[:, None] * n[None, :]) % seq) / seq
    scale = 1.0 / np.sqrt(seq)
    return (np.cos(ang) * scale).astype(np.float32), (-np.sin(ang) * scale).astype(np.float32)


def _dft64_blockdiag():
    n = np.arange(HEAD_DIM)
    ang = 2.0 * np.pi * ((n[:, None] * n[None, :]) % HEAD_DIM) / HEAD_DIM
    c, s = np.cos(ang) / np.sqrt(HEAD_DIM), np.sin(ang) / np.sqrt(HEAD_DIM)
    bd = np.zeros((GROUP, 2 * GROUP), np.float32)
    for h in range(N_HEADS):
        r = slice(h * HEAD_DIM, (h + 1) * HEAD_DIM)
        bd[r, h * HEAD_DIM:(h + 1) * HEAD_DIM] = c
        bd[r, GROUP + h * HEAD_DIM:GROUP + (h + 1) * HEAD_DIM] = s
    return bd


def _block_diag(w):
    n, k, _ = w.shape
    eye = jnp.eye(n, dtype=w.dtype)
    return (eye[:, None, :, None] * w[:, :, None, :]).reshape(n * k, n * k)


def _route_tables(counts, t, tm):
    nt = t // tm + N_CLASSES
    nt += -nt % 8
    ntile_c = (counts + tm - 1) // tm
    tile_end = jnp.cumsum(ntile_c)
    pstart = (tile_end - ntile_c) * tm
    n_used = tile_end[-1]
    tile_ids = jnp.minimum(jnp.arange(nt, dtype=jnp.int32), n_used - 1)
    tile_cls = jnp.sum((tile_ids[:, None] >= tile_end[None, :]).astype(jnp.int32), axis=1)
    pair_lo = jnp.array([p[0] for p in PAIRS], jnp.int32)
    pair_hi = jnp.array([p[1] for p in PAIRS], jnp.int32)
    grp, pid = tile_cls // len(PAIRS), tile_cls % len(PAIRS)
    tile_ea = grp * EXPERTS_PER_GROUP + jnp.take(pair_lo, pid)
    tile_eb = grp * EXPERTS_PER_GROUP + jnp.take(pair_hi, pid)
    pstart = jnp.concatenate([pstart, jnp.zeros((CLS_PAD - N_CLASSES,), I32)])
    return pstart, tile_ea, tile_eb, n_used.reshape(1).astype(jnp.int32), nt


def _trunk(x, mods, lw, shared):
    b, seq, d = x.shape
    t = b * seq
    depth = len(lw)
    for l in range(depth):
        w = lw[l]
        mod = mods[l]
        xg, part = _mix_in(x, mod, w["g1"], w["w_in"], shared["d64"], w["ws"], w["bs"], w["wc"],
                           w["wp"], w["ps"], w["gg"], w["wo_bcd"])
        x_mid, h2, route, cnt = _mix_out(shared["cs"], shared["ss"], xg, part, x, mod, w["wf"],
                                         w["gg"], w["wo_a"], w["g2"], shared["wr_hi"],
                                         shared["wr_lo"], shared["b_router"], shared["tri"])
        cls = route[:, 0, :].reshape(t)
        rank = route[:, 1, :].reshape(t)
        counts = cnt[:N_CLASSES, 0].astype(I32)
        pstart, tile_ea, tile_eb, n_used, nt = _route_tables(counts, t, TM_MOE)
        xs, inv = _sc_dispatch(cls, rank, pstart, h2.reshape(t, W_ROW), nt * TM_MOE)
        ys = _moe_ffn(tile_ea, tile_eb, n_used, xs, w["wg"], w["wu"], w["wd"])
        y_tok = _sc_unsort(ys, inv).reshape(b, seq, W_PACK)
        x = _combine(x_mid, y_tok, mod, shared["g_final"], final=(l == depth - 1))
    return x


def kernel(x_prompt, x_sample, c_prompt, c_sample, w_ada, b_ada, g_norm1, w_in, w_fourier, w_spatial, b_spatial, w_conv, w_pool, pool_scale, g_group, w_out, g_norm2, w_router, b_router, w_exp_gate, w_exp_up, w_exp_down, g_final):
    depth = w_in.shape[0]
    seq = x_prompt.shape[1]
    d = D_MODEL
    nb_p = c_prompt.shape[0]

    cs, ss = _dft_tables(seq)
    wr_hi = w_router.astype(BF16)
    wr_lo = (w_router - wr_hi.astype(F32)).astype(BF16)
    shared = {
        "cs": jnp.asarray(cs).astype(BF16),
        "ss": jnp.asarray(ss).astype(BF16),
        "d64": jnp.asarray(_dft64_blockdiag()).astype(BF16),
        "wr_hi": wr_hi.T, "wr_lo": wr_lo.T,
        "b_router": b_router.reshape(N_EXPERTS, 1).astype(F32),
        "g_final": g_final.reshape(1, d),
        "tri": jnp.asarray(np.triu(np.ones((TM_MIX, TM_MIX), np.float32))).astype(BF16),
    }
    lw = []
    for l in range(depth):
        lw.append({
            "g1": g_norm1[l].reshape(1, d),
            "w_in": w_in[l].astype(BF16),
            "ws": w_spatial[l].reshape(N_HEADS * CHUNK, CHUNK).astype(BF16),
            "bs": jnp.repeat(b_spatial[l].T, HEAD_DIM, axis=1),
            "wc": w_conv[l],
            "wp": _block_diag(w_pool[l]).astype(BF16),
            "ps": pool_scale[l].reshape(1, GROUP),
            "gg": g_group[l].reshape(1, d),
            "wo_bcd": w_out[l, GROUP:].astype(BF16),
            "wo_a": w_out[l, :GROUP].astype(BF16),
            "wf": _block_diag(w_fourier[l]).astype(BF16),
            "g2": g_norm2[l].reshape(1, d),
            "wg": w_exp_gate[l].astype(BF16),
            "wu": w_exp_up[l].astype(BF16),
            "wd": w_exp_down[l].astype(BF16),
        })

    c_all = jnp.concatenate([c_prompt, c_sample], axis=0)
    mod_all = _ada(c_all, w_ada, b_ada).reshape(depth, c_all.shape[0], N_MOD, d)
    mods_p = [mod_all[l, :nb_p] for l in range(depth)]
    mods_s = [mod_all[l, nb_p:] for l in range(depth)]
    y_prompt = _trunk(x_prompt, mods_p, lw, shared)
    y_sample = _trunk(x_sample, mods_s, lw, shared)
    return (y_prompt, y_sample)
```

```python
import functools

import numpy as np
import jax
import jax.numpy as jnp
from jax import lax
from jax.experimental import pallas as pl
from jax.experimental.pallas import tpu as pltpu
from jax.experimental.pallas import tpu_sc as plsc

F32 = jnp.float32
BF16 = jnp.bfloat16
I32 = jnp.int32
U32 = jnp.uint32

D_MODEL = 1024
GROUP = 256
N_HEADS = 4
HEAD_DIM = 64
CHUNK = 128
D_IN = 7 * GROUP
N_EXPERTS = 16
N_EGROUPS = 4
EXPERTS_PER_GROUP = 4
D_EXPERT = 512
N_MOD = 6
EPS = 1e-6
HALO = 8
PAIRS = ((0, 1), (0, 2), (0, 3), (1, 2), (1, 3), (2, 3))
N_CLASSES = N_EGROUPS * len(PAIRS)
CLS_PAD = 32

TM_MIX = 512
TM_MOE = 512
SEQ_PER_STEP = 2
VMEM_LIMIT = 56 * 1024 * 1024

W_PACK = D_MODEL // 2
W_EXTRA = 128
W_ROW = W_PACK + W_EXTRA

SC_CORES = 2
SC_SUBCORES = 16
SC_LANES = 16
SC_WORKERS = SC_CORES * SC_SUBCORES
SC_ROWS = 64
SC_SCAN = 2048


def _rms(x):
    return x * lax.rsqrt(jnp.mean(x * x, axis=-1, keepdims=True) + EPS)


def _dot(a, b):
    return jnp.dot(a, b, preferred_element_type=F32)


def _pack_pairs(x):
    k = x.shape[1] // 2
    xb = x.astype(BF16).astype(F32)
    hi = lax.bitcast_convert_type(xb[:, :k], U32)
    lo = lax.bitcast_convert_type(xb[:, k:], U32)
    return lax.bitcast_convert_type(hi | (lo >> 16), I32)


def _unpack_pairs(w):
    u = lax.bitcast_convert_type(w, U32)
    hi = lax.bitcast_convert_type(u & jnp.uint32(0xFFFF0000), F32)
    lo = lax.bitcast_convert_type(u << 16, F32)
    return jnp.concatenate([hi, lo], axis=1)


def _ada_kernel(c_ref, w_ref, b_ref, o_ref):
    c = c_ref[...]
    sc = c * jax.nn.sigmoid(c)
    o_ref[0] = _dot(sc.astype(BF16), w_ref[0].astype(BF16)) + b_ref[0]


def _ada(c, w_ada, b_ada):
    depth, d, n = w_ada.shape
    bc = c.shape[0]
    tn = 1536
    return pl.pallas_call(
        _ada_kernel,
        grid=(depth, n // tn),
        in_specs=[
            pl.BlockSpec((bc, d), lambda l, j: (0, 0)),
            pl.BlockSpec((1, d, tn), lambda l, j: (l, 0, j)),
            pl.BlockSpec((1, 1, tn), lambda l, j: (l, 0, j)),
        ],
        out_specs=pl.BlockSpec((1, bc, tn), lambda l, j: (l, 0, j)),
        out_shape=jax.ShapeDtypeStruct((depth, bc, n), F32),
        compiler_params=pltpu.CompilerParams(
            dimension_semantics=("arbitrary", "arbitrary"), vmem_limit_bytes=VMEM_LIMIT),
        name="ada_mod",
    )(c, w_ada, b_ada.reshape(depth, 1, n))


def _mix_in_kernel(xm_ref, xp_ref, xn_ref, mod_ref, g1_ref, win_ref, d64_ref, ws_ref, bs_ref,
                   wc_ref, wp_ref, ps_ref, gg_ref, wo_ref, xg_ref, part_ref, *, tm, seq):
    i = pl.program_id(1)
    ne = tm + 2 * HALO
    xe = jnp.concatenate([xp_ref[0], xm_ref[0], xn_ref[0]], axis=0)
    mod = mod_ref[0]
    h = _rms(xe) * g1_ref[...] * (1.0 + mod[1:2]) + mod[0:1]
    proj = _dot(h.astype(BF16), win_ref[...])

    gpos = lax.broadcasted_iota(jnp.int32, (ne, GROUP), 0) + (i * tm - HALO)
    valid = (gpos >= 0) & (gpos < seq)
    pm = proj[HALO:HALO + tm]

    xg_ref[0] = _dot(pm[:, 0:GROUP].astype(BF16), d64_ref[...]).astype(BF16)

    u = pm[:, GROUP:2 * GROUP]
    v = pm[:, 2 * GROUP:3 * GROUP]
    head = lax.broadcasted_iota(jnp.int32, (CHUNK, GROUP), 1) // HEAD_DIM
    yb_chunks = []
    for c in range(tm // CHUNK):
        vc = v[c * CHUNK:(c + 1) * CHUNK].astype(BF16)
        m_all = _dot(ws_ref[...], vc)
        mixed = bs_ref[...]
        for hh in range(N_HEADS):
            mixed = mixed + jnp.where(head == hh, m_all[hh * CHUNK:(hh + 1) * CHUNK], 0.0)
        yb_chunks.append(u[c * CHUNK:(c + 1) * CHUNK] * mixed)
    yb = jnp.concatenate(yb_chunks, axis=0)

    z = jnp.where(valid, proj[:, 4 * GROUP:5 * GROUP] * proj[:, 5 * GROUP:6 * GROUP], 0.0)
    conv = (pltpu.roll(z, 1, 0)[HALO:HALO + tm] * wc_ref[0:1, :]
            + z[HALO:HALO + tm] * wc_ref[1:2, :]
            + pltpu.roll(z, ne - 1, 0)[HALO:HALO + tm] * wc_ref[2:3, :])
    yc = pm[:, 3 * GROUP:4 * GROUP] * conv

    p = jnp.where(valid, proj[:, 6 * GROUP:7 * GROUP], 0.0)
    a2 = p + pltpu.roll(p, 1, 0)
    a4 = a2 + pltpu.roll(a2, 2, 0)
    a8 = a4 + pltpu.roll(a4, 4, 0)
    a16 = a8 + pltpu.roll(a8, 8, 0)
    w2 = a2[HALO:HALO + tm]
    w4 = pltpu.roll(a4, ne - 1, 0)[HALO:HALO + tm]
    w8 = pltpu.roll(a8, ne - 3, 0)[HALO:HALO + tm]
    w16 = pltpu.roll(a16, ne - 7, 0)[HALO:HALO + tm]
    grp = lax.broadcasted_iota(jnp.int32, (tm, GROUP), 1) // HEAD_DIM
    wsum = jnp.where(grp == 0, w2, jnp.where(grp == 1, w4, jnp.where(grp == 2, w8, w16)))
    left = jnp.left_shift(1, grp)
    t = lax.broadcasted_iota(jnp.int32, (tm, GROUP), 0) + i * tm
    cnt = jnp.minimum(t + left, seq) - jnp.maximum(t - left, 0)
    pooled = wsum / cnt.astype(F32) - p[HALO:HALO + tm]
    yd = _dot(pooled.astype(BF16), wp_ref[...]) * ps_ref[...]

    gg = gg_ref[...]
    ycat = jnp.concatenate([
        (_rms(yb) * gg[:, GROUP:2 * GROUP]).astype(BF16),
        (_rms(yc) * gg[:, 2 * GROUP:3 * GROUP]).astype(BF16),
        (_rms(yd) * gg[:, 3 * GROUP:4 * GROUP]).astype(BF16)], axis=1)
    part_ref[0] = _dot(ycat, wo_ref[...]).astype(part_ref.dtype)


def _mix_in(x, mod, g1, w_in, d64, ws, bs, wc, wp, ps, gg, wo_bcd):
    b, seq, d = x.shape
    tm = TM_MIX
    nt = seq // tm
    hb = tm // HALO
    full = lambda *shape: pl.BlockSpec(shape, lambda bi, i: (0,) * len(shape))
    return pl.pallas_call(
        functools.partial(_mix_in_kernel, tm=tm, seq=seq),
        grid=(b, nt),
        in_specs=[
            pl.BlockSpec((1, tm, d), lambda bi, i: (bi, i, 0)),
            pl.BlockSpec((1, HALO, d), lambda bi, i: (bi, jnp.maximum(i * hb - 1, 0), 0)),
            pl.BlockSpec((1, HALO, d), lambda bi, i: (bi, jnp.minimum((i + 1) * hb, seq // HALO - 1), 0)),
            pl.BlockSpec((1, N_MOD, d), lambda bi, i: (bi, 0, 0)),
            full(1, d), full(d, D_IN), full(GROUP, 2 * GROUP), full(N_HEADS * CHUNK, CHUNK),
            full(CHUNK, GROUP), full(3, GROUP), full(GROUP, GROUP), full(1, GROUP),
            full(1, d), full(3 * GROUP, d),
        ],
        out_specs=[
            pl.BlockSpec((1, tm, 2 * GROUP), lambda bi, i: (bi, i, 0)),
            pl.BlockSpec((1, tm, d), lambda bi, i: (bi, i, 0)),
        ],
        out_shape=[
            jax.ShapeDtypeStruct((b, seq, 2 * GROUP), BF16),
            jax.ShapeDtypeStruct((b, seq, d), BF16),
        ],
        compiler_params=pltpu.CompilerParams(
            dimension_semantics=("arbitrary", "arbitrary"), vmem_limit_bytes=VMEM_LIMIT),
        name="mix_in",
    )(x, x, x, mod, g1, w_in, d64, ws, bs, wc, wp, ps, gg, wo_bcd)


def _top2_of4(vals, aux):
    best, bi, ba = vals[0], jnp.zeros_like(vals[0], jnp.int32), aux[0]
    for j in range(1, 4):
        gt = vals[j] > best
        best = jnp.where(gt, vals[j], best)
        bi = jnp.where(gt, j, bi)
        ba = jnp.where(gt, aux[j], ba)
    sec = jnp.full_like(best, -jnp.inf)
    si, sa = jnp.zeros_like(bi), aux[0]
    for j in range(4):
        cand = jnp.where(bi == j, -jnp.inf, vals[j])
        gt = cand > sec
        sec = jnp.where(gt, cand, sec)
        si = jnp.where(gt, j, si)
        sa = jnp.where(gt, aux[j], sa)
    return best, sec, bi, si, ba, sa


def _mix_out_kernel(cs_ref, ss_ref, xg_ref, part_ref, x_ref, mod_ref, wf_ref, gg_ref, wo_ref,
                    g2_ref, wrh_ref, wrl_ref, br_ref, tri_ref, xo_ref, h2_ref, rt_ref, cnt_ref,
                    carry_ref):
    @pl.when((pl.program_id(0) == 0) & (pl.program_id(1) == 0))
    def _():
        carry_ref[...] = jnp.zeros_like(carry_ref)

    nq = xg_ref.shape[0]
    xc = jnp.concatenate([xg_ref[q, :, 0:GROUP] for q in range(nq)], axis=1)
    xs = jnp.concatenate([xg_ref[q, :, GROUP:2 * GROUP] for q in range(nq)], axis=1)
    f_all = _dot(cs_ref[...], xc) + _dot(ss_ref[...], xs)
    for q in range(nq):
        _mix_out_tokens(q, f_all[:, q * GROUP:(q + 1) * GROUP], part_ref, x_ref, mod_ref, wf_ref,
                        gg_ref, wo_ref, g2_ref, wrh_ref, wrl_ref, br_ref, tri_ref, xo_ref, h2_ref,
                        rt_ref, carry_ref)
    cnt_ref[...] = carry_ref[...]


def _mix_out_tokens(q, f, part_ref, x_ref, mod_ref, wf_ref, gg_ref, wo_ref, g2_ref, wrh_ref,
                    wrl_ref, br_ref, tri_ref, xo_ref, h2_ref, rt_ref, carry_ref):
    ya = _dot(f.astype(BF16), wf_ref[...])
    ya = _rms(ya) * gg_ref[:, 0:GROUP]
    mix = _dot(ya.astype(BF16), wo_ref[...]) + part_ref[q].astype(F32)
    mod = mod_ref[q]
    xn = x_ref[q] + mod[2:3] * mix
    xo_ref[q] = xn
    h2 = _rms(xn) * g2_ref[...] * (1.0 + mod[4:5]) + mod[3:4]
    h_hi = h2.astype(BF16)
    h2_ref[q, :, 0:W_PACK] = _pack_pairs(h2)
    h_lo = (h2 - h_hi.astype(F32)).astype(BF16)

    nt_dims = (((1,), (1,)), ((), ()))
    w_hi, w_lo = wrh_ref[...], wrl_ref[...]
    lt = (lax.dot_general(w_hi, h_hi, nt_dims, preferred_element_type=F32)
          + lax.dot_general(w_hi, h_lo, nt_dims, preferred_element_type=F32)
          + lax.dot_general(w_lo, h_hi, nt_dims, preferred_element_type=F32))
    ex = jnp.exp(lt - jnp.max(lt, axis=0, keepdims=True))
    scores = ex / jnp.sum(ex, axis=0, keepdims=True)
    biased = scores + br_ref[...]

    best_gs = None
    for g in range(N_EGROUPS):
        rows = [biased[g * 4 + j:g * 4 + j + 1] for j in range(4)]
        srow = [scores[g * 4 + j:g * 4 + j + 1] for j in range(4)]
        t1, t2, i1, i2, s1, s2 = _top2_of4(rows, srow)
        gs = t1 + t2
        if best_gs is None:
            best_gs, sel = gs, jnp.zeros_like(i1)
            bi1, bi2, bs1, bs2 = i1, i2, s1, s2
        else:
            gt = gs > best_gs
            best_gs = jnp.where(gt, gs, best_gs)
            sel = jnp.where(gt, g, sel)
            bi1, bi2 = jnp.where(gt, i1, bi1), jnp.where(gt, i2, bi2)
            bs1, bs2 = jnp.where(gt, s1, bs1), jnp.where(gt, s2, bs2)
    first_lo = bi1 < bi2
    lo = jnp.where(first_lo, bi1, bi2)
    hi = jnp.where(first_lo, bi2, bi1)
    den = bs1 + bs2
    w_a = jnp.where(first_lo, bs1, bs2) / den
    w_b = jnp.where(first_lo, bs2, bs1) / den
    pair = jnp.where(lo == 0, hi - 1, jnp.where(lo == 1, hi + 1, 5))
    cls_i = sel * len(PAIRS) + pair
    tk = cls_i.shape[1]
    gate_cols = jnp.concatenate([w_a, w_b, jnp.zeros((W_EXTRA - 2, tk), F32)], axis=0).T
    h2_ref[q, :, W_PACK:W_ROW] = lax.bitcast_convert_type(gate_cols, I32)

    onehot = lax.broadcasted_iota(jnp.int32, (CLS_PAD, tk), 0) == cls_i
    prefix = _dot(jnp.where(onehot, 1.0, 0.0).astype(BF16), tri_ref[...])
    carry = carry_ref[...]
    rank = jnp.sum(jnp.where(onehot, prefix - 1.0 + carry[:, 0:1], 0.0), axis=0, keepdims=True)
    carry_ref[...] = carry + prefix[:, tk - 1:tk]
    rt_ref[q] = jnp.concatenate(
        [cls_i, rank.astype(I32), jnp.zeros((6, tk), I32)], axis=0)


def _mix_out(cs, ss, xg, part, x, mod, wf_bd, gg, wo_a, g2, wr_hi, wr_lo, b_router, tri):
    b, seq, d = x.shape
    tk = TM_MIX
    nk = seq // tk
    nq = SEQ_PER_STEP
    full = lambda *shape: pl.BlockSpec(shape, lambda k, bi: (0,) * len(shape))
    return pl.pallas_call(
        _mix_out_kernel,
        grid=(nk, b // nq),
        in_specs=[
            pl.BlockSpec((tk, seq), lambda k, bi: (k, 0)),
            pl.BlockSpec((tk, seq), lambda k, bi: (k, 0)),
            pl.BlockSpec((nq, seq, 2 * GROUP), lambda k, bi: (bi, 0, 0)),
            pl.BlockSpec((nq, tk, d), lambda k, bi: (bi, k, 0)),
            pl.BlockSpec((nq, tk, d), lambda k, bi: (bi, k, 0)),
            pl.BlockSpec((nq, N_MOD, d), lambda k, bi: (bi, 0, 0)),
            full(GROUP, GROUP), full(1, d), full(GROUP, d), full(1, d),
            full(N_EXPERTS, d), full(N_EXPERTS, d), full(N_EXPERTS, 1), full(tk, tk),
        ],
        out_specs=[
            pl.BlockSpec((nq, tk, d), lambda k, bi: (bi, k, 0)),
            pl.BlockSpec((nq, tk, W_ROW), lambda k, bi: (bi, k, 0)),
            pl.BlockSpec((nq, 8, tk), lambda k, bi: (bi, 0, k)),
            full(CLS_PAD, 128),
        ],
        out_shape=[
            jax.ShapeDtypeStruct((b, seq, d), F32),
            jax.ShapeDtypeStruct((b, seq, W_ROW), I32),
            jax.ShapeDtypeStruct((b, 8, seq), I32),
            jax.ShapeDtypeStruct((CLS_PAD, 128), F32),
        ],
        scratch_shapes=[pltpu.VMEM((CLS_PAD, 128), F32)],
        compiler_params=pltpu.CompilerParams(
            dimension_semantics=("arbitrary", "arbitrary"), vmem_limit_bytes=VMEM_LIMIT),
        name="mix_out",
    )(cs, ss, xg, part, x, mod, wf_bd, gg, wo_a, g2, wr_hi, wr_lo, b_router, tri)


def _moe_kernel(ea_ref, eb_ref, nu_ref, xs_ref, wga_ref, wua_ref, wda_ref,
                wgb_ref, wub_ref, wdb_ref, o_ref):
    @pl.when(pl.program_id(0) < nu_ref[0])
    def _():
        xs = _unpack_pairs(xs_ref[:, 0:W_PACK]).astype(BF16)
        wt = lax.bitcast_convert_type(xs_ref[:, W_PACK:W_ROW], F32)

        def ffn(wg, wu, wd):
            a = _dot(xs, wg[0])
            hid = (a * jax.nn.sigmoid(a)) * _dot(xs, wu[0])
            return _dot(hid.astype(BF16), wd[0])

        o_ref[...] = _pack_pairs(wt[:, 0:1] * ffn(wga_ref, wua_ref, wda_ref)
                                 + wt[:, 1:2] * ffn(wgb_ref, wub_ref, wdb_ref))


def _moe_ffn(tile_ea, tile_eb, n_used, xs, wg, wu, wd):
    p = xs.shape[0]
    d = D_MODEL
    tm = TM_MOE
    nt = p // tm
    row = lambda i, ea, eb, nu: (jnp.minimum(i, nu[0] - 1), 0)
    exp_a = lambda i, ea, eb, nu: (ea[i], 0, 0)
    exp_b = lambda i, ea, eb, nu: (eb[i], 0, 0)
    return pl.pallas_call(
        _moe_kernel,
        grid_spec=pltpu.PrefetchScalarGridSpec(
            num_scalar_prefetch=3,
            grid=(nt,),
            in_specs=[
                pl.BlockSpec((tm, W_ROW), row),
                pl.BlockSpec((1, d, D_EXPERT), exp_a),
                pl.BlockSpec((1, d, D_EXPERT), exp_a),
                pl.BlockSpec((1, D_EXPERT, d), exp_a),
                pl.BlockSpec((1, d, D_EXPERT), exp_b),
                pl.BlockSpec((1, d, D_EXPERT), exp_b),
                pl.BlockSpec((1, D_EXPERT, d), exp_b),
            ],
            out_specs=pl.BlockSpec((tm, W_PACK), row),
        ),
        out_shape=jax.ShapeDtypeStruct((p, W_PACK), I32),
        compiler_params=pltpu.CompilerParams(
            dimension_semantics=("arbitrary",), vmem_limit_bytes=VMEM_LIMIT),
        name="moe_ffn",
    )(tile_ea, tile_eb, n_used, xs, wg, wu, wd, wg, wu, wd)


def _sc_worker_id():
    return lax.axis_index("s") * SC_CORES + lax.axis_index("c")


def _sc_gather_rows(table_hbm, out_hbm, idx_all, out_base, n_chunks, idx_bufs, row_bufs, sems):
    def copy(slot):
        return pltpu.make_async_copy(table_hbm.at[idx_bufs[slot]], row_bufs[slot], sems[slot])

    def start(j, slot):
        for q in range(SC_ROWS // SC_LANES):
            idx_bufs[slot][pl.ds(q * SC_LANES, SC_LANES)] = (
                idx_all[pl.ds(j * SC_ROWS + q * SC_LANES, SC_LANES)])
        copy(slot).start()

    def flush(j, slot):
        copy(slot).wait()
        pltpu.sync_copy(row_bufs[slot], out_hbm.at[pl.ds(out_base + j * SC_ROWS, SC_ROWS)])

    start(0, 0)

    @pl.loop(0, n_chunks // 2)
    def _(jj):
        j = 2 * jj
        start(j + 1, 1)
        flush(j, 0)

        @pl.when(j + 2 < n_chunks)
        def _():
            start(j + 2, 0)

        flush(j + 1, 1)


def _sc_scratch(n_idx, width):
    return [
        pltpu.VMEM((n_idx,), I32),
        pltpu.VMEM((SC_ROWS,), I32), pltpu.VMEM((SC_ROWS,), I32),
        pltpu.VMEM((SC_ROWS, width), I32), pltpu.VMEM((SC_ROWS, width), I32),
        pltpu.SemaphoreType.DMA, pltpu.SemaphoreType.DMA,
    ]


def _sc_dispatch(cls, rank, pstart, table, p):
    t, width = table.shape
    pw = p // SC_WORKERS
    n_scan = t // SC_SCAN
    n_chunks = pw // SC_ROWS
    assert p % (SC_WORKERS * 2 * SC_ROWS) == 0 and t % SC_SCAN == 0 and n_scan <= SC_WORKERS
    mesh = plsc.VectorSubcoreMesh(core_axis_name="c", subcore_axis_name="s")

    @functools.partial(
        pl.kernel, mesh=mesh,
        out_type=[jax.ShapeDtypeStruct((p, width), I32), jax.ShapeDtypeStruct((t,), I32)],
        scratch_types=[pltpu.VMEM((SC_SCAN,), I32), pltpu.VMEM((SC_SCAN,), I32),
                       pltpu.VMEM((SC_SCAN,), I32), pltpu.VMEM((CLS_PAD,), I32)]
        + _sc_scratch(pw, width),
        compiler_params=pltpu.CompilerParams(needs_layout_passes=False),
        name="dispatch")
    def k(cls_hbm, rank_hbm, ps_hbm, table_hbm, xs_hbm, inv_hbm,
          cls_v, rank_v, inv_v, ps_v, src_v, idx_a, idx_b, rows_a, rows_b, sem_a, sem_b):
        wid = _sc_worker_id()
        lo = wid * pw
        pltpu.sync_copy(ps_hbm, ps_v)

        @pl.loop(0, pw // SC_LANES)
        def _(i):
            src_v[pl.ds(i * SC_LANES, SC_LANES)] = jnp.zeros((SC_LANES,), I32)

        @pl.loop(0, n_scan)
        def _(c):
            pltpu.sync_copy(cls_hbm.at[pl.ds(c * SC_SCAN, SC_SCAN)], cls_v)
            pltpu.sync_copy(rank_hbm.at[pl.ds(c * SC_SCAN, SC_SCAN)], rank_v)

            @pl.loop(0, SC_SCAN // SC_LANES)
            def _(i):
                sl = pl.ds(i * SC_LANES, SC_LANES)
                pos = plsc.load_gather(ps_v, [cls_v[sl]]) + rank_v[sl]
                inv_v[sl] = pos
                tok = c * SC_SCAN + i * SC_LANES + lax.iota(I32, SC_LANES)
                mine = (pos >= lo) & (pos < lo + pw)
                plsc.store_scatter(src_v, [pos - lo], tok, mask=mine)

            @pl.when(c == wid)
            def _():
                pltpu.sync_copy(inv_v, inv_hbm.at[pl.ds(c * SC_SCAN, SC_SCAN)])

        _sc_gather_rows(table_hbm, xs_hbm, src_v, lo, n_chunks,
                        (idx_a, idx_b), (rows_a, rows_b), (sem_a, sem_b))

    return k(cls, rank, pstart, table)


def _sc_unsort(table, inv):
    t = inv.shape[0]
    width = table.shape[1]
    tw = t // SC_WORKERS
    n_chunks = tw // SC_ROWS
    assert t % (SC_WORKERS * 2 * SC_ROWS) == 0
    mesh = plsc.VectorSubcoreMesh(core_axis_name="c", subcore_axis_name="s")

    @functools.partial(
        pl.kernel, mesh=mesh, out_type=jax.ShapeDtypeStruct((t, width), I32),
        scratch_types=_sc_scratch(tw, width),
        compiler_params=pltpu.CompilerParams(needs_layout_passes=False),
        name="unsort")
    def k(table_hbm, inv_hbm, out_hbm, idx_all, idx_a, idx_b, rows_a, rows_b, sem_a, sem_b):
        base = _sc_worker_id() * tw
        pltpu.sync_copy(inv_hbm.at[pl.ds(base, tw)], idx_all)
        _sc_gather_rows(table_hbm, out_hbm, idx_all, base, n_chunks,
                        (idx_a, idx_b), (rows_a, rows_b), (sem_a, sem_b))

    return k(table, inv)


def _combine_kernel(x_ref, y_ref, mod_ref, gf_ref, o_ref, *, final):
    xn = x_ref[0] + mod_ref[0][5:6] * _unpack_pairs(y_ref[0])
    if final:
        xn = _rms(xn) * gf_ref[...]
    o_ref[0] = xn


def _combine(x, y, mod, g_final, final):
    b, seq, d = x.shape
    tm = TM_MIX
    blk = pl.BlockSpec((1, tm, d), lambda bi, i: (bi, i, 0))
    return pl.pallas_call(
        functools.partial(_combine_kernel, final=final),
        grid=(b, seq // tm),
        in_specs=[blk, pl.BlockSpec((1, tm, W_PACK), lambda bi, i: (bi, i, 0)),
                  pl.BlockSpec((1, N_MOD, d), lambda bi, i: (bi, 0, 0)),
                  pl.BlockSpec((1, d), lambda bi, i: (0, 0))],
        out_specs=blk,
        out_shape=jax.ShapeDtypeStruct((b, seq, d), F32),
        compiler_params=pltpu.CompilerParams(
            dimension_semantics=("arbitrary", "arbitrary"), vmem_limit_bytes=VMEM_LIMIT),
        name="combine",
    )(x, y, mod, g_final)


def _dft_tables(seq):
    n = np.arange(seq)
    ang = 2.0 * np.pi * ((n[:, None] * n[None, :]) % seq) / seq
    scale = 1.0 / np.sqrt(seq)
    return (np.cos(ang) * scale).astype(np.float32), (-np.sin(ang) * scale).astype(np.float32)


def _dft64_blockdiag():
    n = np.arange(HEAD_DIM)
    ang = 2.0 * np.pi * ((n[:, None] * n[None, :]) % HEAD_DIM) / HEAD_DIM
    c, s = np.cos(ang) / np.sqrt(HEAD_DIM), np.sin(ang) / np.sqrt(HEAD_DIM)
    bd = np.zeros((GROUP, 2 * GROUP), np.float32)
    for h in range(N_HEADS):
        r = slice(h * HEAD_DIM, (h + 1) * HEAD_DIM)
        bd[r, h * HEAD_DIM:(h + 1) * HEAD_DIM] = c
        bd[r, GROUP + h * HEAD_DIM:GROUP + (h + 1) * HEAD_DIM] = s
    return bd


def _block_diag(w):
    n, k, _ = w.shape
    eye = jnp.eye(n, dtype=w.dtype)
    return (eye[:, None, :, None] * w[:, :, None, :]).reshape(n * k, n * k)


def _route_tables(counts, t, tm):
    nt = t // tm + N_CLASSES
    nt += -nt % 8
    ntile_c = (counts + tm - 1) // tm
    tile_end = jnp.cumsum(ntile_c)
    pstart = (tile_end - ntile_c) * tm
    n_used = tile_end[-1]
    tile_ids = jnp.minimum(jnp.arange(nt, dtype=jnp.int32), n_used - 1)
    tile_cls = jnp.sum((tile_ids[:, None] >= tile_end[None, :]).astype(jnp.int32), axis=1)
    pair_lo = jnp.array([p[0] for p in PAIRS], jnp.int32)
    pair_hi = jnp.array([p[1] for p in PAIRS], jnp.int32)
    grp, pid = tile_cls // len(PAIRS), tile_cls % len(PAIRS)
    tile_ea = grp * EXPERTS_PER_GROUP + jnp.take(pair_lo, pid)
    tile_eb = grp * EXPERTS_PER_GROUP + jnp.take(pair_hi, pid)
    pstart = jnp.concatenate([pstart, jnp.zeros((CLS_PAD - N_CLASSES,), I32)])
    return pstart, tile_ea, tile_eb, n_used.reshape(1).astype(jnp.int32), nt


def _trunk(x, mods, lw, shared):
    b, seq, d = x.shape
    t = b * seq
    depth = len(lw)
    for l in range(depth):
        w = lw[l]
        mod = mods[l]
        xg, part = _mix_in(x, mod, w["g1"], w["w_in"], shared["d64"], w["ws"], w["bs"], w["wc"],
                           w["wp"], w["ps"], w["gg"], w["wo_bcd"])
        x_mid, h2, route, cnt = _mix_out(shared["cs"], shared["ss"], xg, part, x, mod, w["wf"],
                                         w["gg"], w["wo_a"], w["g2"], shared["wr_hi"],
                                         shared["wr_lo"], shared["b_router"], shared["tri"])
        cls = route[:, 0, :].reshape(t)
        rank = route[:, 1, :].reshape(t)
        counts = cnt[:N_CLASSES, 0].astype(I32)
        pstart, tile_ea, tile_eb, n_used, nt = _route_tables(counts, t, TM_MOE)
        xs, inv = _sc_dispatch(cls, rank, pstart, h2.reshape(t, W_ROW), nt * TM_MOE)
        ys = _moe_ffn(tile_ea, tile_eb, n_used, xs, w["wg"], w["wu"], w["wd"])
        y_tok = _sc_unsort(ys, inv).reshape(b, seq, W_PACK)
        x = _combine(x_mid, y_tok, mod, shared["g_final"], final=(l == depth - 1))
    return x


def kernel(x_prompt, x_sample, c_prompt, c_sample, w_ada, b_ada, g_norm1, w_in, w_fourier, w_spatial, b_spatial, w_conv, w_pool, pool_scale, g_group, w_out, g_norm2, w_router, b_router, w_exp_gate, w_exp_up, w_exp_down, g_final):
    depth = w_in.shape[0]
    seq = x_prompt.shape[1]
    d = D_MODEL
    nb_p = c_prompt.shape[0]

    cs, ss = _dft_tables(seq)
    wr_hi = w_router.astype(BF16)
    wr_lo = (w_router - wr_hi.astype(F32)).astype(BF16)
    shared = {
        "cs": jnp.asarray(cs).astype(BF16),
        "ss": jnp.asarray(ss).astype(BF16),
        "d64": jnp.asarray(_dft64_blockdiag()).astype(BF16),
        "wr_hi": wr_hi.T, "wr_lo": wr_lo.T,
        "b_router": b_router.reshape(N_EXPERTS, 1).astype(F32),
        "g_final": g_final.reshape(1, d),
        "tri": jnp.asarray(np.triu(np.ones((TM_MIX, TM_MIX), np.float32))).astype(BF16),
    }
    lw = []
    for l in range(depth):
        lw.append({
            "g1": g_norm1[l].reshape(1, d),
            "w_in": w_in[l].astype(BF16),
            "ws": w_spatial[l].reshape(N_HEADS * CHUNK, CHUNK).astype(BF16),
            "bs": jnp.repeat(b_spatial[l].T, HEAD_DIM, axis=1),
            "wc": w_conv[l],
            "wp": _block_diag(w_pool[l]).astype(BF16),
            "ps": pool_scale[l].reshape(1, GROUP),
            "gg": g_group[l].reshape(1, d),
            "wo_bcd": w_out[l, GROUP:].astype(BF16),
            "wo_a": w_out[l, :GROUP].astype(BF16),
            "wf": _block_diag(w_fourier[l]).astype(BF16),
            "g2": g_norm2[l].reshape(1, d),
            "wg": w_exp_gate[l].astype(BF16),
            "wu": w_exp_up[l].astype(BF16),
            "wd": w_exp_down[l].astype(BF16),
        })

    c_all = jnp.concatenate([c_prompt, c_sample], axis=0)
    mod_all = _ada(c_all, w_ada, b_ada).reshape(depth, c_all.shape[0], N_MOD, d)
    mods_p = [mod_all[l, :nb_p] for l in range(depth)]
    mods_s = [mod_all[l, nb_p:] for l in range(depth)]
    y_prompt = _trunk(x_prompt, mods_p, lw, shared)
    y_sample = _trunk(x_sample, mods_s, lw, shared)
    return (y_prompt, y_sample)
```

```python
import functools

import numpy as np
import jax
import jax.numpy as jnp
from jax import lax
from jax.experimental import pallas as pl
from jax.experimental.pallas import tpu as pltpu
from jax.experimental.pallas import tpu_sc as plsc

F32 = jnp.float32
BF16 = jnp.bfloat16
I32 = jnp.int32
U32 = jnp.uint32

D_MODEL = 1024
GROUP = 256
N_HEADS = 4
HEAD_DIM = 64
CHUNK = 128
D_IN = 7 * GROUP
N_EXPERTS = 16
N_EGROUPS = 4
EXPERTS_PER_GROUP = 4
D_EXPERT = 512
N_MOD = 6
EPS = 1e-6
HALO = 8
PAIRS = ((0, 1), (0, 2), (0, 3), (1, 2), (1, 3), (2, 3))
N_CLASSES = N_EGROUPS * len(PAIRS)
CLS_PAD = 32

TM_MIX = 512
TM_MOE = 512
SEQ_PER_STEP = 2
TM_COMBINE = 1024
VMEM_LIMIT = 56 * 1024 * 1024

W_PACK = D_MODEL // 2
W_EXTRA = 128
W_ROW = W_PACK + W_EXTRA

SC_CORES = 2
SC_SUBCORES = 16
SC_LANES = 16
SC_WORKERS = SC_CORES * SC_SUBCORES
SC_ROWS = 64
SC_SCAN = 2048


def _rms(x):
    return x * lax.rsqrt(jnp.mean(x * x, axis=-1, keepdims=True) + EPS)


def _dot(a, b):
    return jnp.dot(a, b, preferred_element_type=F32)


def _pack_pairs(x):
    k = x.shape[1] // 2
    xb = x.astype(BF16).astype(F32)
    hi = lax.bitcast_convert_type(xb[:, :k], U32)
    lo = lax.bitcast_convert_type(xb[:, k:], U32)
    return lax.bitcast_convert_type(hi | (lo >> 16), I32)


def _unpack_pairs(w):
    u = lax.bitcast_convert_type(w, U32)
    hi = lax.bitcast_convert_type(u & jnp.uint32(0xFFFF0000), F32)
    lo = lax.bitcast_convert_type(u << 16, F32)
    return jnp.concatenate([hi, lo], axis=1)


def _ada_kernel(c_ref, w_ref, b_ref, o_ref):
    c = c_ref[...]
    sc = c * jax.nn.sigmoid(c)
    o_ref[0] = _dot(sc.astype(BF16), w_ref[0].astype(BF16)) + b_ref[0]


def _ada(c, w_ada, b_ada):
    depth, d, n = w_ada.shape
    bc = c.shape[0]
    tn = 1536
    return pl.pallas_call(
        _ada_kernel,
        grid=(depth, n // tn),
        in_specs=[
            pl.BlockSpec((bc, d), lambda l, j: (0, 0)),
            pl.BlockSpec((1, d, tn), lambda l, j: (l, 0, j)),
            pl.BlockSpec((1, 1, tn), lambda l, j: (l, 0, j)),
        ],
        out_specs=pl.BlockSpec((1, bc, tn), lambda l, j: (l, 0, j)),
        out_shape=jax.ShapeDtypeStruct((depth, bc, n), F32),
        compiler_params=pltpu.CompilerParams(
            dimension_semantics=("arbitrary", "arbitrary"), vmem_limit_bytes=VMEM_LIMIT),
        name="ada_mod",
    )(c, w_ada, b_ada.reshape(depth, 1, n))


def _mix_in_kernel(xm_ref, xp_ref, xn_ref, mod_ref, g1_ref, win_ref, d64_ref, ws_ref, bs_ref,
                   wc_ref, wp_ref, ps_ref, gg_ref, wo_ref, xg_ref, part_ref, *, tm, seq):
    i = pl.program_id(1)
    ne = tm + 2 * HALO
    xe = jnp.concatenate([xp_ref[0], xm_ref[0], xn_ref[0]], axis=0)
    mod = mod_ref[0]
    h = _rms(xe) * g1_ref[...] * (1.0 + mod[1:2]) + mod[0:1]
    proj = _dot(h.astype(BF16), win_ref[...])

    gpos = lax.broadcasted_iota(jnp.int32, (ne, GROUP), 0) + (i * tm - HALO)
    valid = (gpos >= 0) & (gpos < seq)
    pm = proj[HALO:HALO + tm]

    xg_ref[0] = _dot(pm[:, 0:GROUP].astype(BF16), d64_ref[...]).astype(BF16)

    u = pm[:, GROUP:2 * GROUP]
    v = pm[:, 2 * GROUP:3 * GROUP]
    head = lax.broadcasted_iota(jnp.int32, (CHUNK, GROUP), 1) // HEAD_DIM
    yb_chunks = []
    for c in range(tm // CHUNK):
        vc = v[c * CHUNK:(c + 1) * CHUNK].astype(BF16)
        m_all = _dot(ws_ref[...], vc)
        mixed = bs_ref[...]
        for hh in range(N_HEADS):
            mixed = mixed + jnp.where(head == hh, m_all[hh * CHUNK:(hh + 1) * CHUNK], 0.0)
        yb_chunks.append(u[c * CHUNK:(c + 1) * CHUNK] * mixed)
    yb = jnp.concatenate(yb_chunks, axis=0)

    z = jnp.where(valid, proj[:, 4 * GROUP:5 * GROUP] * proj[:, 5 * GROUP:6 * GROUP], 0.0)
    conv = (pltpu.roll(z, 1, 0)[HALO:HALO + tm] * wc_ref[0:1, :]
            + z[HALO:HALO + tm] * wc_ref[1:2, :]
            + pltpu.roll(z, ne - 1, 0)[HALO:HALO + tm] * wc_ref[2:3, :])
    yc = pm[:, 3 * GROUP:4 * GROUP] * conv

    p = jnp.where(valid, proj[:, 6 * GROUP:7 * GROUP], 0.0)
    a2 = p + pltpu.roll(p, 1, 0)
    a4 = a2 + pltpu.roll(a2, 2, 0)
    a8 = a4 + pltpu.roll(a4, 4, 0)
    a16 = a8 + pltpu.roll(a8, 8, 0)
    w2 = a2[HALO:HALO + tm]
    w4 = pltpu.roll(a4, ne - 1, 0)[HALO:HALO + tm]
    w8 = pltpu.roll(a8, ne - 3, 0)[HALO:HALO + tm]
    w16 = pltpu.roll(a16, ne - 7, 0)[HALO:HALO + tm]
    grp = lax.broadcasted_iota(jnp.int32, (tm, GROUP), 1) // HEAD_DIM
    wsum = jnp.where(grp == 0, w2, jnp.where(grp == 1, w4, jnp.where(grp == 2, w8, w16)))
    left = jnp.left_shift(1, grp)
    t = lax.broadcasted_iota(jnp.int32, (tm, GROUP), 0) + i * tm
    cnt = jnp.minimum(t + left, seq) - jnp.maximum(t - left, 0)
    pooled = wsum / cnt.astype(F32) - p[HALO:HALO + tm]
    yd = _dot(pooled.astype(BF16), wp_ref[...]) * ps_ref[...]

    gg = gg_ref[...]
    ycat = jnp.concatenate([
        (_rms(yb) * gg[:, GROUP:2 * GROUP]).astype(BF16),
        (_rms(yc) * gg[:, 2 * GROUP:3 * GROUP]).astype(BF16),
        (_rms(yd) * gg[:, 3 * GROUP:4 * GROUP]).astype(BF16)], axis=1)
    part_ref[0] = _dot(ycat, wo_ref[...]).astype(part_ref.dtype)


def _mix_in(x, mod, g1, w_in, d64, ws, bs, wc, wp, ps, gg, wo_bcd):
    b, seq, d = x.shape
    tm = TM_MIX
    nt = seq // tm
    hb = tm // HALO
    full = lambda *shape: pl.BlockSpec(shape, lambda bi, i: (0,) * len(shape))
    return pl.pallas_call(
        functools.partial(_mix_in_kernel, tm=tm, seq=seq),
        grid=(b, nt),
        in_specs=[
            pl.BlockSpec((1, tm, d), lambda bi, i: (bi, i, 0)),
            pl.BlockSpec((1, HALO, d), lambda bi, i: (bi, jnp.maximum(i * hb - 1, 0), 0)),
            pl.BlockSpec((1, HALO, d), lambda bi, i: (bi, jnp.minimum((i + 1) * hb, seq // HALO - 1), 0)),
            pl.BlockSpec((1, N_MOD, d), lambda bi, i: (bi, 0, 0)),
            full(1, d), full(d, D_IN), full(GROUP, 2 * GROUP), full(N_HEADS * CHUNK, CHUNK),
            full(CHUNK, GROUP), full(3, GROUP), full(GROUP, GROUP), full(1, GROUP),
            full(1, d), full(3 * GROUP, d),
        ],
        out_specs=[
            pl.BlockSpec((1, tm, 2 * GROUP), lambda bi, i: (bi, i, 0)),
            pl.BlockSpec((1, tm, d), lambda bi, i: (bi, i, 0)),
        ],
        out_shape=[
            jax.ShapeDtypeStruct((b, seq, 2 * GROUP), BF16),
            jax.ShapeDtypeStruct((b, seq, d), BF16),
        ],
        compiler_params=pltpu.CompilerParams(
            dimension_semantics=("arbitrary", "arbitrary"), vmem_limit_bytes=VMEM_LIMIT),
        name="mix_in",
    )(x, x, x, mod, g1, w_in, d64, ws, bs, wc, wp, ps, gg, wo_bcd)


def _top2_of4(vals, aux):
    best, bi, ba = vals[0], jnp.zeros_like(vals[0], jnp.int32), aux[0]
    for j in range(1, 4):
        gt = vals[j] > best
        best = jnp.where(gt, vals[j], best)
        bi = jnp.where(gt, j, bi)
        ba = jnp.where(gt, aux[j], ba)
    sec = jnp.full_like(best, -jnp.inf)
    si, sa = jnp.zeros_like(bi), aux[0]
    for j in range(4):
        cand = jnp.where(bi == j, -jnp.inf, vals[j])
        gt = cand > sec
        sec = jnp.where(gt, cand, sec)
        si = jnp.where(gt, j, si)
        sa = jnp.where(gt, aux[j], sa)
    return best, sec, bi, si, ba, sa


def _mix_out_kernel(cs_ref, ss_ref, xg_ref, part_ref, x_ref, mod_ref, wf_ref, gg_ref, wo_ref,
                    g2_ref, wrh_ref, wrl_ref, br_ref, tri_ref, xo_ref, h2_ref, cls_ref, rk_ref,
                    cnt_ref,
                    carry_ref):
    @pl.when((pl.program_id(0) == 0) & (pl.program_id(1) == 0))
    def _():
        carry_ref[...] = jnp.zeros_like(carry_ref)

    nq = xg_ref.shape[0]
    xc = jnp.concatenate([xg_ref[q, :, 0:GROUP] for q in range(nq)], axis=1)
    xs = jnp.concatenate([xg_ref[q, :, GROUP:2 * GROUP] for q in range(nq)], axis=1)
    f_all = _dot(cs_ref[...], xc) + _dot(ss_ref[...], xs)
    for q in range(nq):
        _mix_out_tokens(q, f_all[:, q * GROUP:(q + 1) * GROUP], part_ref, x_ref, mod_ref, wf_ref,
                        gg_ref, wo_ref, g2_ref, wrh_ref, wrl_ref, br_ref, tri_ref, xo_ref, h2_ref,
                        cls_ref, rk_ref, carry_ref)
    cnt_ref[...] = carry_ref[...]


def _mix_out_tokens(q, f, part_ref, x_ref, mod_ref, wf_ref, gg_ref, wo_ref, g2_ref, wrh_ref,
                    wrl_ref, br_ref, tri_ref, xo_ref, h2_ref, cls_ref, rk_ref, carry_ref):
    ya = _dot(f.astype(BF16), wf_ref[...])
    ya = _rms(ya) * gg_ref[:, 0:GROUP]
    mix = _dot(ya.astype(BF16), wo_ref[...]) + part_ref[q].astype(F32)
    mod = mod_ref[q]
    xn = x_ref[q] + mod[2:3] * mix
    xo_ref[q] = xn
    h2 = _rms(xn) * g2_ref[...] * (1.0 + mod[4:5]) + mod[3:4]
    h_hi = h2.astype(BF16)
    h2_ref[q, :, 0:W_PACK] = _pack_pairs(h2)
    h_lo = (h2 - h_hi.astype(F32)).astype(BF16)

    nt_dims = (((1,), (1,)), ((), ()))
    w_hi, w_lo = wrh_ref[...], wrl_ref[...]
    lt = (lax.dot_general(w_hi, h_hi, nt_dims, preferred_element_type=F32)
          + lax.dot_general(w_hi, h_lo, nt_dims, preferred_element_type=F32)
          + lax.dot_general(w_lo, h_hi, nt_dims, preferred_element_type=F32))
    ex = jnp.exp(lt - jnp.max(lt, axis=0, keepdims=True))
    scores = ex / jnp.sum(ex, axis=0, keepdims=True)
    biased = scores + br_ref[...]

    best_gs = None
    for g in range(N_EGROUPS):
        rows = [biased[g * 4 + j:g * 4 + j + 1] for j in range(4)]
        srow = [scores[g * 4 + j:g * 4 + j + 1] for j in range(4)]
        t1, t2, i1, i2, s1, s2 = _top2_of4(rows, srow)
        gs = t1 + t2
        if best_gs is None:
            best_gs, sel = gs, jnp.zeros_like(i1)
            bi1, bi2, bs1, bs2 = i1, i2, s1, s2
        else:
            gt = gs > best_gs
            best_gs = jnp.where(gt, gs, best_gs)
            sel = jnp.where(gt, g, sel)
            bi1, bi2 = jnp.where(gt, i1, bi1), jnp.where(gt, i2, bi2)
            bs1, bs2 = jnp.where(gt, s1, bs1), jnp.where(gt, s2, bs2)
    first_lo = bi1 < bi2
    lo = jnp.where(first_lo, bi1, bi2)
    hi = jnp.where(first_lo, bi2, bi1)
    den = bs1 + bs2
    w_a = jnp.where(first_lo, bs1, bs2) / den
    w_b = jnp.where(first_lo, bs2, bs1) / den
    pair = jnp.where(lo == 0, hi - 1, jnp.where(lo == 1, hi + 1, 5))
    cls_i = sel * len(PAIRS) + pair
    tk = cls_i.shape[1]
    gate_cols = jnp.concatenate([w_a, w_b, jnp.zeros((W_EXTRA - 2, tk), F32)], axis=0).T
    h2_ref[q, :, W_PACK:W_ROW] = lax.bitcast_convert_type(gate_cols, I32)

    onehot = lax.broadcasted_iota(jnp.int32, (CLS_PAD, tk), 0) == cls_i
    prefix = _dot(jnp.where(onehot, 1.0, 0.0).astype(BF16), tri_ref[...])
    carry = carry_ref[...]
    rank = jnp.sum(jnp.where(onehot, prefix - 1.0 + carry[:, 0:1], 0.0), axis=0, keepdims=True)
    carry_ref[...] = carry + prefix[:, tk - 1:tk]
    cls_ref[q] = cls_i
    rk_ref[q] = rank.astype(I32)


def _mix_out(cs, ss, xg, part, x, mod, wf_bd, gg, wo_a, g2, wr_hi, wr_lo, b_router, tri):
    b, seq, d = x.shape
    tk = TM_MIX
    nk = seq // tk
    nq = SEQ_PER_STEP
    full = lambda *shape: pl.BlockSpec(shape, lambda k, bi: (0,) * len(shape))
    return pl.pallas_call(
        _mix_out_kernel,
        grid=(nk, b // nq),
        in_specs=[
            pl.BlockSpec((tk, seq), lambda k, bi: (k, 0)),
            pl.BlockSpec((tk, seq), lambda k, bi: (k, 0)),
            pl.BlockSpec((nq, seq, 2 * GROUP), lambda k, bi: (bi, 0, 0)),
            pl.BlockSpec((nq, tk, d), lambda k, bi: (bi, k, 0)),
            pl.BlockSpec((nq, tk, d), lambda k, bi: (bi, k, 0)),
            pl.BlockSpec((nq, N_MOD, d), lambda k, bi: (bi, 0, 0)),
            full(GROUP, GROUP), full(1, d), full(GROUP, d), full(1, d),
            full(N_EXPERTS, d), full(N_EXPERTS, d), full(N_EXPERTS, 1), full(tk, tk),
        ],
        out_specs=[
            pl.BlockSpec((nq, tk, d), lambda k, bi: (bi, k, 0)),
            pl.BlockSpec((nq, tk, W_ROW), lambda k, bi: (bi, k, 0)),
            pl.BlockSpec((nq, 1, tk), lambda k, bi: (bi, 0, k)),
            pl.BlockSpec((nq, 1, tk), lambda k, bi: (bi, 0, k)),
            full(CLS_PAD, 128),
        ],
        out_shape=[
            jax.ShapeDtypeStruct((b, seq, d), F32),
            jax.ShapeDtypeStruct((b, seq, W_ROW), I32),
            jax.ShapeDtypeStruct((b, 1, seq), I32),
            jax.ShapeDtypeStruct((b, 1, seq), I32),
            jax.ShapeDtypeStruct((CLS_PAD, 128), F32),
        ],
        scratch_shapes=[pltpu.VMEM((CLS_PAD, 128), F32)],
        compiler_params=pltpu.CompilerParams(
            dimension_semantics=("arbitrary", "arbitrary"), vmem_limit_bytes=VMEM_LIMIT),
        name="mix_out",
    )(cs, ss, xg, part, x, mod, wf_bd, gg, wo_a, g2, wr_hi, wr_lo, b_router, tri)


def _moe_kernel(ea_ref, eb_ref, nu_ref, xs_ref, wga_ref, wua_ref, wda_ref,
                wgb_ref, wub_ref, wdb_ref, o_ref):
    @pl.when(pl.program_id(0) < nu_ref[0])
    def _():
        xs = _unpack_pairs(xs_ref[:, 0:W_PACK]).astype(BF16)
        wt = lax.bitcast_convert_type(xs_ref[:, W_PACK:W_ROW], F32)

        def ffn(wg, wu, wd):
            a = _dot(xs, wg[0])
            hid = (a * jax.nn.sigmoid(a)) * _dot(xs, wu[0])
            return _dot(hid.astype(BF16), wd[0])

        o_ref[...] = _pack_pairs(wt[:, 0:1] * ffn(wga_ref, wua_ref, wda_ref)
                                 + wt[:, 1:2] * ffn(wgb_ref, wub_ref, wdb_ref))


def _moe_ffn(tile_ea, tile_eb, n_used, xs, wg, wu, wd):
    p = xs.shape[0]
    d = D_MODEL
    tm = TM_MOE
    nt = p // tm
    row = lambda i, ea, eb, nu: (jnp.minimum(i, nu[0] - 1), 0)
    exp_a = lambda i, ea, eb, nu: (ea[i], 0, 0)
    exp_b = lambda i, ea, eb, nu: (eb[i], 0, 0)
    return pl.pallas_call(
        _moe_kernel,
        grid_spec=pltpu.PrefetchScalarGridSpec(
            num_scalar_prefetch=3,
            grid=(nt,),
            in_specs=[
                pl.BlockSpec((tm, W_ROW), row),
                pl.BlockSpec((1, d, D_EXPERT), exp_a),
                pl.BlockSpec((1, d, D_EXPERT), exp_a),
                pl.BlockSpec((1, D_EXPERT, d), exp_a),
                pl.BlockSpec((1, d, D_EXPERT), exp_b),
                pl.BlockSpec((1, d, D_EXPERT), exp_b),
                pl.BlockSpec((1, D_EXPERT, d), exp_b),
            ],
            out_specs=pl.BlockSpec((tm, W_PACK), row),
        ),
        out_shape=jax.ShapeDtypeStruct((p, W_PACK), I32),
        compiler_params=pltpu.CompilerParams(
            dimension_semantics=("arbitrary",), vmem_limit_bytes=VMEM_LIMIT),
        name="moe_ffn",
    )(tile_ea, tile_eb, n_used, xs, wg, wu, wd, wg, wu, wd)


def _sc_worker_id():
    return lax.axis_index("s") * SC_CORES + lax.axis_index("c")


def _sc_gather_rows(table_hbm, out_hbm, idx_all, out_base, n_chunks, idx_bufs, row_bufs, sems):
    def copy(slot):
        return pltpu.make_async_copy(table_hbm.at[idx_bufs[slot]], row_bufs[slot], sems[slot])

    def start(j, slot):
        for q in range(SC_ROWS // SC_LANES):
            idx_bufs[slot][pl.ds(q * SC_LANES, SC_LANES)] = (
                idx_all[pl.ds(j * SC_ROWS + q * SC_LANES, SC_LANES)])
        copy(slot).start()

    def flush(j, slot):
        copy(slot).wait()
        pltpu.sync_copy(row_bufs[slot], out_hbm.at[pl.ds(out_base + j * SC_ROWS, SC_ROWS)])

    start(0, 0)

    @pl.loop(0, n_chunks // 2)
    def _(jj):
        j = 2 * jj
        start(j + 1, 1)
        flush(j, 0)

        @pl.when(j + 2 < n_chunks)
        def _():
            start(j + 2, 0)

        flush(j + 1, 1)


def _sc_scratch(n_idx, width):
    return [
        pltpu.VMEM((n_idx,), I32),
        pltpu.VMEM((SC_ROWS,), I32), pltpu.VMEM((SC_ROWS,), I32),
        pltpu.VMEM((SC_ROWS, width), I32), pltpu.VMEM((SC_ROWS, width), I32),
        pltpu.SemaphoreType.DMA, pltpu.SemaphoreType.DMA,
    ]


def _sc_dispatch(cls, rank, pstart, table, p):
    t, width = table.shape
    tw = t // SC_WORKERS
    n_chunks = tw // SC_ROWS
    assert t % (SC_WORKERS * 2 * SC_ROWS) == 0
    mesh = plsc.VectorSubcoreMesh(core_axis_name="c", subcore_axis_name="s")

    @functools.partial(
        pl.kernel, mesh=mesh,
        out_type=[jax.ShapeDtypeStruct((p, width), I32), jax.ShapeDtypeStruct((t,), I32)],
        scratch_types=[pltpu.VMEM((tw,), I32), pltpu.VMEM((CLS_PAD,), I32)]
        + _sc_scratch(tw, width) + [pltpu.SemaphoreType.DMA, pltpu.SemaphoreType.DMA],
        compiler_params=pltpu.CompilerParams(needs_layout_passes=False),
        name="dispatch")
    def k(cls_hbm, rank_hbm, ps_hbm, table_hbm, xs_hbm, inv_hbm,
          cls_v, ps_v, inv_v, idx_a, idx_b, rows_a, rows_b, rsem_a, rsem_b, wsem_a, wsem_b):
        base = _sc_worker_id() * tw
        pltpu.sync_copy(ps_hbm, ps_v)
        pltpu.sync_copy(cls_hbm.at[pl.ds(base, tw)], cls_v)
        pltpu.sync_copy(rank_hbm.at[pl.ds(base, tw)], inv_v)

        @pl.loop(0, tw // SC_LANES)
        def _(i):
            sl = pl.ds(i * SC_LANES, SC_LANES)
            inv_v[sl] = plsc.load_gather(ps_v, [cls_v[sl]]) + inv_v[sl]

        pltpu.sync_copy(inv_v, inv_hbm.at[pl.ds(base, tw)])

        slots = ((idx_a, rows_a, rsem_a, wsem_a), (idx_b, rows_b, rsem_b, wsem_b))

        def read(j, slot):
            _, rows, rsem, _ = slots[slot]
            return pltpu.make_async_copy(
                table_hbm.at[pl.ds(base + j * SC_ROWS, SC_ROWS)], rows, rsem)

        def write(slot):
            idx, rows, _, wsem = slots[slot]
            return pltpu.make_async_copy(rows, xs_hbm.at[idx], wsem)

        @pl.loop(0, n_chunks // 2)
        def _(jj):
            for slot in range(2):
                j = 2 * jj + slot
                read(j, slot).start()
                for q in range(SC_ROWS // SC_LANES):
                    slots[slot][0][pl.ds(q * SC_LANES, SC_LANES)] = (
                        inv_v[pl.ds(j * SC_ROWS + q * SC_LANES, SC_LANES)])
            for slot in range(2):
                read(2 * jj + slot, slot).wait()
                write(slot).start()
            for slot in range(2):
                write(slot).wait()

    return k(cls, rank, pstart, table)


def _sc_unsort(table, inv):
    t = inv.shape[0]
    width = table.shape[1]
    tw = t // SC_WORKERS
    n_chunks = tw // SC_ROWS
    assert t % (SC_WORKERS * 2 * SC_ROWS) == 0
    mesh = plsc.VectorSubcoreMesh(core_axis_name="c", subcore_axis_name="s")

    @functools.partial(
        pl.kernel, mesh=mesh, out_type=jax.ShapeDtypeStruct((t, width), I32),
        scratch_types=_sc_scratch(tw, width),
        compiler_params=pltpu.CompilerParams(needs_layout_passes=False),
        name="unsort")
    def k(table_hbm, inv_hbm, out_hbm, idx_all, idx_a, idx_b, rows_a, rows_b, sem_a, sem_b):
        base = _sc_worker_id() * tw
        pltpu.sync_copy(inv_hbm.at[pl.ds(base, tw)], idx_all)
        _sc_gather_rows(table_hbm, out_hbm, idx_all, base, n_chunks,
                        (idx_a, idx_b), (rows_a, rows_b), (sem_a, sem_b))

    return k(table, inv)


def _combine_kernel(x_ref, y_ref, mod_ref, gf_ref, o_ref, *, final):
    xn = x_ref[0] + mod_ref[0][5:6] * _unpack_pairs(y_ref[0])
    if final:
        xn = _rms(xn) * gf_ref[...]
    o_ref[0] = xn


def _combine(x, y, mod, g_final, final):
    b, seq, d = x.shape
    tm = TM_COMBINE
    blk = pl.BlockSpec((1, tm, d), lambda bi, i: (bi, i, 0))
    return pl.pallas_call(
        functools.partial(_combine_kernel, final=final),
        grid=(b, seq // tm),
        in_specs=[blk, pl.BlockSpec((1, tm, W_PACK), lambda bi, i: (bi, i, 0)),
                  pl.BlockSpec((1, N_MOD, d), lambda bi, i: (bi, 0, 0)),
                  pl.BlockSpec((1, d), lambda bi, i: (0, 0))],
        out_specs=blk,
        out_shape=jax.ShapeDtypeStruct((b, seq, d), F32),
        compiler_params=pltpu.CompilerParams(
            dimension_semantics=("arbitrary", "arbitrary"), vmem_limit_bytes=VMEM_LIMIT),
        name="combine",
    )(x, y, mod, g_final)


def _dft_tables(seq):
    n = np.arange(seq)
    ang = 2.0 * np.pi * ((n[:, None] * n[None, :]) % seq) / seq
    scale = 1.0 / np.sqrt(seq)
    return (np.cos(ang) * scale).astype(np.float32), (-np.sin(ang) * scale).astype(np.float32)


def _dft64_blockdiag():
    n = np.arange(HEAD_DIM)
    ang = 2.0 * np.pi * ((n[:, None] * n[None, :]) % HEAD_DIM) / HEAD_DIM
    c, s = np.cos(ang) / np.sqrt(HEAD_DIM), np.sin(ang) / np.sqrt(HEAD_DIM)
    bd = np.zeros((GROUP, 2 * GROUP), np.float32)
    for h in range(N_HEADS):
        r = slice(h * HEAD_DIM, (h + 1) * HEAD_DIM)
        bd[r, h * HEAD_DIM:(h + 1) * HEAD_DIM] = c
        bd[r, GROUP + h * HEAD_DIM:GROUP + (h + 1) * HEAD_DIM] = s
    return bd


def _block_diag(w):
    n, k, _ = w.shape
    eye = jnp.eye(n, dtype=w.dtype)
    return (eye[:, None, :, None] * w[:, :, None, :]).reshape(n * k, n * k)


def _route_tables(counts, t, tm):
    nt = t // tm + N_CLASSES
    nt += -nt % 8
    ntile_c = (counts + tm - 1) // tm
    tile_end = jnp.cumsum(ntile_c)
    pstart = (tile_end - ntile_c) * tm
    n_used = tile_end[-1]
    tile_ids = jnp.minimum(jnp.arange(nt, dtype=jnp.int32), n_used - 1)
    tile_cls = jnp.sum((tile_ids[:, None] >= tile_end[None, :]).astype(jnp.int32), axis=1)
    pair_lo = jnp.array([p[0] for p in PAIRS], jnp.int32)
    pair_hi = jnp.array([p[1] for p in PAIRS], jnp.int32)
    grp, pid = tile_cls // len(PAIRS), tile_cls % len(PAIRS)
    tile_ea = grp * EXPERTS_PER_GROUP + jnp.take(pair_lo, pid)
    tile_eb = grp * EXPERTS_PER_GROUP + jnp.take(pair_hi, pid)
    pstart = jnp.concatenate([pstart, jnp.zeros((CLS_PAD - N_CLASSES,), I32)])
    return pstart, tile_ea, tile_eb, n_used.reshape(1).astype(jnp.int32), nt


def _trunk(x, mods, lw, shared):
    b, seq, d = x.shape
    t = b * seq
    depth = len(lw)
    for l in range(depth):
        w = lw[l]
        mod = mods[l]
        xg, part = _mix_in(x, mod, w["g1"], w["w_in"], shared["d64"], w["ws"], w["bs"], w["wc"],
                           w["wp"], w["ps"], w["gg"], w["wo_bcd"])
        x_mid, h2, cls, rank, cnt = _mix_out(
            shared["cs"], shared["ss"], xg, part, x, mod, w["wf"], w["gg"], w["wo_a"], w["g2"],
            shared["wr_hi"], shared["wr_lo"], shared["b_router"], shared["tri"])
        counts = cnt[:N_CLASSES, 0].astype(I32)
        pstart, tile_ea, tile_eb, n_used, nt = _route_tables(counts, t, TM_MOE)
        xs, inv = _sc_dispatch(cls.reshape(t), rank.reshape(t), pstart, h2.reshape(t, W_ROW),
                               nt * TM_MOE)
        ys = _moe_ffn(tile_ea + l * N_EXPERTS, tile_eb + l * N_EXPERTS, n_used, xs,
                      shared["wg"], shared["wu"], shared["wd"])
        y_tok = _sc_unsort(ys, inv).reshape(b, seq, W_PACK)
        x = _combine(x_mid, y_tok, mod, shared["g_final"], final=(l == depth - 1))
    return x


def kernel(x_prompt, x_sample, c_prompt, c_sample, w_ada, b_ada, g_norm1, w_in, w_fourier, w_spatial, b_spatial, w_conv, w_pool, pool_scale, g_group, w_out, g_norm2, w_router, b_router, w_exp_gate, w_exp_up, w_exp_down, g_final):
    depth = w_in.shape[0]
    seq = x_prompt.shape[1]
    d = D_MODEL
    nb_p = c_prompt.shape[0]

    cs, ss = _dft_tables(seq)
    wr_hi = w_router.astype(BF16)
    wr_lo = (w_router - wr_hi.astype(F32)).astype(BF16)
    shared = {
        "cs": jnp.asarray(cs).astype(BF16),
        "ss": jnp.asarray(ss).astype(BF16),
        "d64": jnp.asarray(_dft64_blockdiag()).astype(BF16),
        "wr_hi": wr_hi.T, "wr_lo": wr_lo.T,
        "b_router": b_router.reshape(N_EXPERTS, 1).astype(F32),
        "g_final": g_final.reshape(1, d),
        "tri": jnp.asarray(np.triu(np.ones((TM_MIX, TM_MIX), np.float32))).astype(BF16),
    }
    lw = []
    for l in range(depth):
        lw.append({
            "g1": g_norm1[l].reshape(1, d),
            "w_in": w_in[l].astype(BF16),
            "ws": w_spatial[l].reshape(N_HEADS * CHUNK, CHUNK).astype(BF16),
            "bs": jnp.repeat(b_spatial[l].T, HEAD_DIM, axis=1),
            "wc": w_conv[l],
            "wp": _block_diag(w_pool[l]).astype(BF16),
            "ps": pool_scale[l].reshape(1, GROUP),
            "gg": g_group[l].reshape(1, d),
            "wo_bcd": w_out[l, GROUP:].astype(BF16),
            "wo_a": w_out[l, :GROUP].astype(BF16),
            "wf": _block_diag(w_fourier[l]).astype(BF16),
            "g2": g_norm2[l].reshape(1, d),
        })
    shared["wg"] = w_exp_gate.astype(BF16).reshape(depth * N_EXPERTS, d, D_EXPERT)
    shared["wu"] = w_exp_up.astype(BF16).reshape(depth * N_EXPERTS, d, D_EXPERT)
    shared["wd"] = w_exp_down.astype(BF16).reshape(depth * N_EXPERTS, D_EXPERT, d)

    c_all = jnp.concatenate([c_prompt, c_sample], axis=0)
    mod_all = _ada(c_all, w_ada, b_ada).reshape(depth, c_all.shape[0], N_MOD, d)
    mods_p = [mod_all[l, :nb_p] for l in range(depth)]
    mods_s = [mod_all[l, nb_p:] for l in range(depth)]
    y_prompt = _trunk(x_prompt, mods_p, lw, shared)
    y_sample = _trunk(x_sample, mods_s, lw, shared)
    return (y_prompt, y_sample)
```

```python
import functools

import numpy as np
import jax
import jax.numpy as jnp
from jax import lax
from jax.experimental import pallas as pl
from jax.experimental.pallas import tpu as pltpu
from jax.experimental.pallas import tpu_sc as plsc

F32 = jnp.float32
BF16 = jnp.bfloat16
I32 = jnp.int32
U32 = jnp.uint32

D_MODEL = 1024
GROUP = 256
N_HEADS = 4
HEAD_DIM = 64
CHUNK = 128
D_IN = 7 * GROUP
N_EXPERTS = 16
N_EGROUPS = 4
EXPERTS_PER_GROUP = 4
D_EXPERT = 512
N_MOD = 6
EPS = 1e-6
HALO = 8
PAIRS = ((0, 1), (0, 2), (0, 3), (1, 2), (1, 3), (2, 3))
N_CLASSES = N_EGROUPS * len(PAIRS)
CLS_PAD = 32

TM_MIX = 512
TM_MOE = 512
SEQ_PER_STEP = 2
TM_COMBINE = 1024
VMEM_LIMIT = 56 * 1024 * 1024

W_PACK = D_MODEL // 2
W_EXTRA = 128
W_ROW = W_PACK + W_EXTRA

SC_CORES = 2
SC_SUBCORES = 16
SC_LANES = 16
SC_WORKERS = SC_CORES * SC_SUBCORES
SC_ROWS = 64
SC_SCAN = 2048


def _rms(x):
    return x * lax.rsqrt(jnp.mean(x * x, axis=-1, keepdims=True) + EPS)


def _dot(a, b):
    return jnp.dot(a, b, preferred_element_type=F32)


def _pack_pairs(x):
    k = x.shape[1] // 2
    xb = x.astype(BF16).astype(F32)
    hi = lax.bitcast_convert_type(xb[:, :k], U32)
    lo = lax.bitcast_convert_type(xb[:, k:], U32)
    return lax.bitcast_convert_type(hi | (lo >> 16), I32)


def _unpack_pairs(w):
    u = lax.bitcast_convert_type(w, U32)
    hi = lax.bitcast_convert_type(u & jnp.uint32(0xFFFF0000), F32)
    lo = lax.bitcast_convert_type(u << 16, F32)
    return jnp.concatenate([hi, lo], axis=1)


def _ada_kernel(c_ref, w_ref, b_ref, o_ref):
    c = c_ref[...]
    sc = c * jax.nn.sigmoid(c)
    o_ref[0] = _dot(sc.astype(BF16), w_ref[0].astype(BF16)) + b_ref[0]


def _ada(c, w_ada, b_ada):
    depth, d, n = w_ada.shape
    bc = c.shape[0]
    tn = 1536
    return pl.pallas_call(
        _ada_kernel,
        grid=(depth, n // tn),
        in_specs=[
            pl.BlockSpec((bc, d), lambda l, j: (0, 0)),
            pl.BlockSpec((1, d, tn), lambda l, j: (l, 0, j)),
            pl.BlockSpec((1, 1, tn), lambda l, j: (l, 0, j)),
        ],
        out_specs=pl.BlockSpec((1, bc, tn), lambda l, j: (l, 0, j)),
        out_shape=jax.ShapeDtypeStruct((depth, bc, n), F32),
        compiler_params=pltpu.CompilerParams(
            dimension_semantics=("arbitrary", "arbitrary"), vmem_limit_bytes=VMEM_LIMIT),
        name="ada_mod",
    )(c, w_ada, b_ada.reshape(depth, 1, n))


def _mix_in_kernel(*refs, tm, seq, fused):
    if fused:
        ym_ref, yp_ref, yn_ref, modp_ref = refs[3:7]
        refs = refs[:3] + refs[7:]
    (xm_ref, xp_ref, xn_ref, mod_ref, g1_ref, win_ref, d64_ref, ws_ref, bs_ref,
     wc_ref, wp_ref, ps_ref, gg_ref, wo_ref, xg_ref, part_ref) = refs
    i = pl.program_id(1)
    ne = tm + 2 * HALO
    xe = jnp.concatenate([xp_ref[0], xm_ref[0], xn_ref[0]], axis=0)
    if fused:
        ye = jnp.concatenate([yp_ref[0], ym_ref[0], yn_ref[0]], axis=0)
        xe = xe + modp_ref[0][5:6] * _unpack_pairs(ye)
    mod = mod_ref[0]
    h = _rms(xe) * g1_ref[...] * (1.0 + mod[1:2]) + mod[0:1]
    proj = _dot(h.astype(BF16), win_ref[...])

    gpos = lax.broadcasted_iota(jnp.int32, (ne, GROUP), 0) + (i * tm - HALO)
    valid = (gpos >= 0) & (gpos < seq)
    pm = proj[HALO:HALO + tm]

    xg_ref[0] = _dot(pm[:, 0:GROUP].astype(BF16), d64_ref[...]).astype(BF16)

    u = pm[:, GROUP:2 * GROUP]
    v = pm[:, 2 * GROUP:3 * GROUP]
    head = lax.broadcasted_iota(jnp.int32, (CHUNK, GROUP), 1) // HEAD_DIM
    yb_chunks = []
    for c in range(tm // CHUNK):
        vc = v[c * CHUNK:(c + 1) * CHUNK].astype(BF16)
        m_all = _dot(ws_ref[...], vc)
        mixed = bs_ref[...]
        for hh in range(N_HEADS):
            mixed = mixed + jnp.where(head == hh, m_all[hh * CHUNK:(hh + 1) * CHUNK], 0.0)
        yb_chunks.append(u[c * CHUNK:(c + 1) * CHUNK] * mixed)
    yb = jnp.concatenate(yb_chunks, axis=0)

    z = jnp.where(valid, proj[:, 4 * GROUP:5 * GROUP] * proj[:, 5 * GROUP:6 * GROUP], 0.0)
    conv = (pltpu.roll(z, 1, 0)[HALO:HALO + tm] * wc_ref[0:1, :]
            + z[HALO:HALO + tm] * wc_ref[1:2, :]
            + pltpu.roll(z, ne - 1, 0)[HALO:HALO + tm] * wc_ref[2:3, :])
    yc = pm[:, 3 * GROUP:4 * GROUP] * conv

    p = jnp.where(valid, proj[:, 6 * GROUP:7 * GROUP], 0.0)
    a2 = p + pltpu.roll(p, 1, 0)
    a4 = a2 + pltpu.roll(a2, 2, 0)
    a8 = a4 + pltpu.roll(a4, 4, 0)
    a16 = a8 + pltpu.roll(a8, 8, 0)
    w2 = a2[HALO:HALO + tm]
    w4 = pltpu.roll(a4, ne - 1, 0)[HALO:HALO + tm]
    w8 = pltpu.roll(a8, ne - 3, 0)[HALO:HALO + tm]
    w16 = pltpu.roll(a16, ne - 7, 0)[HALO:HALO + tm]
    grp = lax.broadcasted_iota(jnp.int32, (tm, GROUP), 1) // HEAD_DIM
    wsum = jnp.where(grp == 0, w2, jnp.where(grp == 1, w4, jnp.where(grp == 2, w8, w16)))
    left = jnp.left_shift(1, grp)
    t = lax.broadcasted_iota(jnp.int32, (tm, GROUP), 0) + i * tm
    cnt = jnp.minimum(t + left, seq) - jnp.maximum(t - left, 0)
    pooled = wsum / cnt.astype(F32) - p[HALO:HALO + tm]
    yd = _dot(pooled.astype(BF16), wp_ref[...]) * ps_ref[...]

    gg = gg_ref[...]
    ycat = jnp.concatenate([
        (_rms(yb) * gg[:, GROUP:2 * GROUP]).astype(BF16),
        (_rms(yc) * gg[:, 2 * GROUP:3 * GROUP]).astype(BF16),
        (_rms(yd) * gg[:, 3 * GROUP:4 * GROUP]).astype(BF16)], axis=1)
    part_ref[0] = _dot(ycat, wo_ref[...]).astype(part_ref.dtype)


def _mix_in(x, prev, mod, g1, w_in, d64, ws, bs, wc, wp, ps, gg, wo_bcd):
    b, seq, d = x.shape
    tm = TM_MIX
    nt = seq // tm
    hb = tm // HALO
    full = lambda *shape: pl.BlockSpec(shape, lambda bi, i: (0,) * len(shape))
    mod_spec = pl.BlockSpec((1, N_MOD, d), lambda bi, i: (bi, 0, 0))

    def rows(width):
        return [
            pl.BlockSpec((1, tm, width), lambda bi, i: (bi, i, 0)),
            pl.BlockSpec((1, HALO, width), lambda bi, i: (bi, jnp.maximum(i * hb - 1, 0), 0)),
            pl.BlockSpec((1, HALO, width),
                         lambda bi, i: (bi, jnp.minimum((i + 1) * hb, seq // HALO - 1), 0)),
        ]

    fused = prev is not None
    prev_specs = rows(W_PACK) + [mod_spec] if fused else []
    prev_args = (prev[0], prev[0], prev[0], prev[1]) if fused else ()
    return pl.pallas_call(
        functools.partial(_mix_in_kernel, tm=tm, seq=seq, fused=fused),
        grid=(b, nt),
        in_specs=rows(d) + prev_specs + [
            mod_spec,
            full(1, d), full(d, D_IN), full(GROUP, 2 * GROUP), full(N_HEADS * CHUNK, CHUNK),
            full(CHUNK, GROUP), full(3, GROUP), full(GROUP, GROUP), full(1, GROUP),
            full(1, d), full(3 * GROUP, d),
        ],
        out_specs=[
            pl.BlockSpec((1, tm, 2 * GROUP), lambda bi, i: (bi, i, 0)),
            pl.BlockSpec((1, tm, d), lambda bi, i: (bi, i, 0)),
        ],
        out_shape=[
            jax.ShapeDtypeStruct((b, seq, 2 * GROUP), BF16),
            jax.ShapeDtypeStruct((b, seq, d), BF16),
        ],
        compiler_params=pltpu.CompilerParams(
            dimension_semantics=("arbitrary", "arbitrary"), vmem_limit_bytes=VMEM_LIMIT),
        name="mix_in",
    )(x, x, x, *prev_args, mod, g1, w_in, d64, ws, bs, wc, wp, ps, gg, wo_bcd)


def _top2_of4(vals, aux):
    best, bi, ba = vals[0], jnp.zeros_like(vals[0], jnp.int32), aux[0]
    for j in range(1, 4):
        gt = vals[j] > best
        best = jnp.where(gt, vals[j], best)
        bi = jnp.where(gt, j, bi)
        ba = jnp.where(gt, aux[j], ba)
    sec = jnp.full_like(best, -jnp.inf)
    si, sa = jnp.zeros_like(bi), aux[0]
    for j in range(4):
        cand = jnp.where(bi == j, -jnp.inf, vals[j])
        gt = cand > sec
        sec = jnp.where(gt, cand, sec)
        si = jnp.where(gt, j, si)
        sa = jnp.where(gt, aux[j], sa)
    return best, sec, bi, si, ba, sa


def _mix_out_kernel(*refs, fused):
    y_ref = modp_ref = None
    if fused:
        y_ref, modp_ref = refs[5:7]
        refs = refs[:5] + refs[7:]
    (cs_ref, ss_ref, xg_ref, part_ref, x_ref, mod_ref, wf_ref, gg_ref, wo_ref,
     g2_ref, wrh_ref, wrl_ref, br_ref, tri_ref, xo_ref, h2_ref, cls_ref, rk_ref,
     cnt_ref, carry_ref) = refs

    @pl.when((pl.program_id(0) == 0) & (pl.program_id(1) == 0))
    def _():
        carry_ref[...] = jnp.zeros_like(carry_ref)

    nq = xg_ref.shape[0]
    xc = jnp.concatenate([xg_ref[q, :, 0:GROUP] for q in range(nq)], axis=1)
    xs = jnp.concatenate([xg_ref[q, :, GROUP:2 * GROUP] for q in range(nq)], axis=1)
    f_all = _dot(cs_ref[...], xc) + _dot(ss_ref[...], xs)
    for q in range(nq):
        x_in = x_ref[q]
        if fused:
            x_in = x_in + modp_ref[q][5:6] * _unpack_pairs(y_ref[q])
        _mix_out_tokens(q, f_all[:, q * GROUP:(q + 1) * GROUP], x_in, part_ref, mod_ref, wf_ref,
                        gg_ref, wo_ref, g2_ref, wrh_ref, wrl_ref, br_ref, tri_ref, xo_ref, h2_ref,
                        cls_ref, rk_ref, carry_ref)
    cnt_ref[...] = carry_ref[...]


def _mix_out_tokens(q, f, x_in, part_ref, mod_ref, wf_ref, gg_ref, wo_ref, g2_ref, wrh_ref,
                    wrl_ref, br_ref, tri_ref, xo_ref, h2_ref, cls_ref, rk_ref, carry_ref):
    ya = _dot(f.astype(BF16), wf_ref[...])
    ya = _rms(ya) * gg_ref[:, 0:GROUP]
    mix = _dot(ya.astype(BF16), wo_ref[...]) + part_ref[q].astype(F32)
    mod = mod_ref[q]
    xn = x_in + mod[2:3] * mix
    xo_ref[q] = xn
    h2 = _rms(xn) * g2_ref[...] * (1.0 + mod[4:5]) + mod[3:4]
    h_hi = h2.astype(BF16)
    h2_ref[q, :, 0:W_PACK] = _pack_pairs(h2)
    h_lo = (h2 - h_hi.astype(F32)).astype(BF16)

    nt_dims = (((1,), (1,)), ((), ()))
    w_hi, w_lo = wrh_ref[...], wrl_ref[...]
    lt = (lax.dot_general(w_hi, h_hi, nt_dims, preferred_element_type=F32)
          + lax.dot_general(w_hi, h_lo, nt_dims, preferred_element_type=F32)
          + lax.dot_general(w_lo, h_hi, nt_dims, preferred_element_type=F32))
    ex = jnp.exp(lt - jnp.max(lt, axis=0, keepdims=True))
    scores = ex / jnp.sum(ex, axis=0, keepdims=True)
    biased = scores + br_ref[...]

    best_gs = None
    for g in range(N_EGROUPS):
        rows = [biased[g * 4 + j:g * 4 + j + 1] for j in range(4)]
        srow = [scores[g * 4 + j:g * 4 + j + 1] for j in range(4)]
        t1, t2, i1, i2, s1, s2 = _top2_of4(rows, srow)
        gs = t1 + t2
        if best_gs is None:
            best_gs, sel = gs, jnp.zeros_like(i1)
            bi1, bi2, bs1, bs2 = i1, i2, s1, s2
        else:
            gt = gs > best_gs
            best_gs = jnp.where(gt, gs, best_gs)
            sel = jnp.where(gt, g, sel)
            bi1, bi2 = jnp.where(gt, i1, bi1), jnp.where(gt, i2, bi2)
            bs1, bs2 = jnp.where(gt, s1, bs1), jnp.where(gt, s2, bs2)
    first_lo = bi1 < bi2
    lo = jnp.where(first_lo, bi1, bi2)
    hi = jnp.where(first_lo, bi2, bi1)
    den = bs1 + bs2
    w_a = jnp.where(first_lo, bs1, bs2) / den
    w_b = jnp.where(first_lo, bs2, bs1) / den
    pair = jnp.where(lo == 0, hi - 1, jnp.where(lo == 1, hi + 1, 5))
    cls_i = sel * len(PAIRS) + pair
    tk = cls_i.shape[1]
    gate_cols = jnp.concatenate([w_a, w_b, jnp.zeros((W_EXTRA - 2, tk), F32)], axis=0).T
    h2_ref[q, :, W_PACK:W_ROW] = lax.bitcast_convert_type(gate_cols, I32)

    onehot = lax.broadcasted_iota(jnp.int32, (CLS_PAD, tk), 0) == cls_i
    prefix = _dot(jnp.where(onehot, 1.0, 0.0).astype(BF16), tri_ref[...])
    carry = carry_ref[...]
    rank = jnp.sum(jnp.where(onehot, prefix - 1.0 + carry[:, 0:1], 0.0), axis=0, keepdims=True)
    carry_ref[...] = carry + prefix[:, tk - 1:tk]
    cls_ref[q] = cls_i
    rk_ref[q] = rank.astype(I32)


def _mix_out(cs, ss, xg, part, x, prev, mod, wf_bd, gg, wo_a, g2, wr_hi, wr_lo, b_router, tri):
    b, seq, d = x.shape
    tk = TM_MIX
    nk = seq // tk
    nq = SEQ_PER_STEP
    full = lambda *shape: pl.BlockSpec(shape, lambda k, bi: (0,) * len(shape))
    mod_spec = pl.BlockSpec((nq, N_MOD, d), lambda k, bi: (bi, 0, 0))
    fused = prev is not None
    prev_specs = [pl.BlockSpec((nq, tk, W_PACK), lambda k, bi: (bi, k, 0)), mod_spec] if fused else []
    prev_args = tuple(prev) if fused else ()
    return pl.pallas_call(
        functools.partial(_mix_out_kernel, fused=fused),
        grid=(nk, b // nq),
        in_specs=[
            pl.BlockSpec((tk, seq), lambda k, bi: (k, 0)),
            pl.BlockSpec((tk, seq), lambda k, bi: (k, 0)),
            pl.BlockSpec((nq, seq, 2 * GROUP), lambda k, bi: (bi, 0, 0)),
            pl.BlockSpec((nq, tk, d), lambda k, bi: (bi, k, 0)),
            pl.BlockSpec((nq, tk, d), lambda k, bi: (bi, k, 0)),
        ] + prev_specs + [
            mod_spec,
            full(GROUP, GROUP), full(1, d), full(GROUP, d), full(1, d),
            full(N_EXPERTS, d), full(N_EXPERTS, d), full(N_EXPERTS, 1), full(tk, tk),
        ],
        out_specs=[
            pl.BlockSpec((nq, tk, d), lambda k, bi: (bi, k, 0)),
            pl.BlockSpec((nq, tk, W_ROW), lambda k, bi: (bi, k, 0)),
            pl.BlockSpec((nq, 1, tk), lambda k, bi: (bi, 0, k)),
            pl.BlockSpec((nq, 1, tk), lambda k, bi: (bi, 0, k)),
            full(CLS_PAD, 128),
        ],
        out_shape=[
            jax.ShapeDtypeStruct((b, seq, d), F32),
            jax.ShapeDtypeStruct((b, seq, W_ROW), I32),
            jax.ShapeDtypeStruct((b, 1, seq), I32),
            jax.ShapeDtypeStruct((b, 1, seq), I32),
            jax.ShapeDtypeStruct((CLS_PAD, 128), F32),
        ],
        scratch_shapes=[pltpu.VMEM((CLS_PAD, 128), F32)],
        compiler_params=pltpu.CompilerParams(
            dimension_semantics=("arbitrary", "arbitrary"), vmem_limit_bytes=VMEM_LIMIT),
        name="mix_out",
    )(cs, ss, xg, part, x, *prev_args, mod, wf_bd, gg, wo_a, g2, wr_hi, wr_lo, b_router, tri)


def _moe_kernel(ea_ref, eb_ref, nu_ref, xs_ref, wga_ref, wua_ref, wda_ref,
                wgb_ref, wub_ref, wdb_ref, o_ref):
    @pl.when(pl.program_id(0) < nu_ref[0])
    def _():
        xs = _unpack_pairs(xs_ref[:, 0:W_PACK]).astype(BF16)
        wt = lax.bitcast_convert_type(xs_ref[:, W_PACK:W_ROW], F32)

        def ffn(wg, wu, wd):
            a = _dot(xs, wg[0])
            hid = (a * jax.nn.sigmoid(a)) * _dot(xs, wu[0])
            return _dot(hid.astype(BF16), wd[0])

        o_ref[...] = _pack_pairs(wt[:, 0:1] * ffn(wga_ref, wua_ref, wda_ref)
                                 + wt[:, 1:2] * ffn(wgb_ref, wub_ref, wdb_ref))


def _moe_ffn(tile_ea, tile_eb, n_used, xs, wg, wu, wd):
    p = xs.shape[0]
    d = D_MODEL
    tm = TM_MOE
    nt = p // tm
    row = lambda i, ea, eb, nu: (jnp.minimum(i, nu[0] - 1), 0)
    exp_a = lambda i, ea, eb, nu: (ea[i], 0, 0)
    exp_b = lambda i, ea, eb, nu: (eb[i], 0, 0)
    return pl.pallas_call(
        _moe_kernel,
        grid_spec=pltpu.PrefetchScalarGridSpec(
            num_scalar_prefetch=3,
            grid=(nt,),
            in_specs=[
                pl.BlockSpec((tm, W_ROW), row),
                pl.BlockSpec((1, d, D_EXPERT), exp_a),
                pl.BlockSpec((1, d, D_EXPERT), exp_a),
                pl.BlockSpec((1, D_EXPERT, d), exp_a),
                pl.BlockSpec((1, d, D_EXPERT), exp_b),
                pl.BlockSpec((1, d, D_EXPERT), exp_b),
                pl.BlockSpec((1, D_EXPERT, d), exp_b),
            ],
            out_specs=pl.BlockSpec((tm, W_PACK), row),
        ),
        out_shape=jax.ShapeDtypeStruct((p, W_PACK), I32),
        compiler_params=pltpu.CompilerParams(
            dimension_semantics=("arbitrary",), vmem_limit_bytes=VMEM_LIMIT),
        name="moe_ffn",
    )(tile_ea, tile_eb, n_used, xs, wg, wu, wd, wg, wu, wd)


def _sc_worker_id():
    return lax.axis_index("s") * SC_CORES + lax.axis_index("c")


def _sc_gather_rows(table_hbm, out_hbm, idx_all, out_base, n_chunks, idx_bufs, row_bufs, sems):
    def copy(slot):
        return pltpu.make_async_copy(table_hbm.at[idx_bufs[slot]], row_bufs[slot], sems[slot])

    def start(j, slot):
        for q in range(SC_ROWS // SC_LANES):
            idx_bufs[slot][pl.ds(q * SC_LANES, SC_LANES)] = (
                idx_all[pl.ds(j * SC_ROWS + q * SC_LANES, SC_LANES)])
        copy(slot).start()

    def flush(j, slot):
        copy(slot).wait()
        pltpu.sync_copy(row_bufs[slot], out_hbm.at[pl.ds(out_base + j * SC_ROWS, SC_ROWS)])

    start(0, 0)

    @pl.loop(0, n_chunks // 2)
    def _(jj):
        j = 2 * jj
        start(j + 1, 1)
        flush(j, 0)

        @pl.when(j + 2 < n_chunks)
        def _():
            start(j + 2, 0)

        flush(j + 1, 1)


def _sc_scratch(n_idx, width):
    return [
        pltpu.VMEM((n_idx,), I32),
        pltpu.VMEM((SC_ROWS,), I32), pltpu.VMEM((SC_ROWS,), I32),
        pltpu.VMEM((SC_ROWS, width), I32), pltpu.VMEM((SC_ROWS, width), I32),
        pltpu.SemaphoreType.DMA, pltpu.SemaphoreType.DMA,
    ]


def _sc_dispatch(cls, rank, pstart, table, p):
    t, width = table.shape
    tw = t // SC_WORKERS
    n_chunks = tw // SC_ROWS
    assert t % (SC_WORKERS * 2 * SC_ROWS) == 0
    mesh = plsc.VectorSubcoreMesh(core_axis_name="c", subcore_axis_name="s")

    @functools.partial(
        pl.kernel, mesh=mesh,
        out_type=[jax.ShapeDtypeStruct((p, width), I32), jax.ShapeDtypeStruct((t,), I32)],
        scratch_types=[pltpu.VMEM((tw,), I32), pltpu.VMEM((CLS_PAD,), I32)]
        + _sc_scratch(tw, width) + [pltpu.SemaphoreType.DMA, pltpu.SemaphoreType.DMA],
        compiler_params=pltpu.CompilerParams(needs_layout_passes=False),
        name="dispatch")
    def k(cls_hbm, rank_hbm, ps_hbm, table_hbm, xs_hbm, inv_hbm,
          cls_v, ps_v, inv_v, idx_a, idx_b, rows_a, rows_b, rsem_a, rsem_b, wsem_a, wsem_b):
        base = _sc_worker_id() * tw
        pltpu.sync_copy(ps_hbm, ps_v)
        pltpu.sync_copy(cls_hbm.at[pl.ds(base, tw)], cls_v)
        pltpu.sync_copy(rank_hbm.at[pl.ds(base, tw)], inv_v)

        @pl.loop(0, tw // SC_LANES)
        def _(i):
            sl = pl.ds(i * SC_LANES, SC_LANES)
            inv_v[sl] = plsc.load_gather(ps_v, [cls_v[sl]]) + inv_v[sl]

        pltpu.sync_copy(inv_v, inv_hbm.at[pl.ds(base, tw)])

        slots = ((idx_a, rows_a, rsem_a, wsem_a), (idx_b, rows_b, rsem_b, wsem_b))

        def read(j, slot):
            _, rows, rsem, _ = slots[slot]
            return pltpu.make_async_copy(
                table_hbm.at[pl.ds(base + j * SC_ROWS, SC_ROWS)], rows, rsem)

        def write(slot):
            idx, rows, _, wsem = slots[slot]
            return pltpu.make_async_copy(rows, xs_hbm.at[idx], wsem)

        @pl.loop(0, n_chunks // 2)
        def _(jj):
            for slot in range(2):
                j = 2 * jj + slot
                read(j, slot).start()
                for q in range(SC_ROWS // SC_LANES):
                    slots[slot][0][pl.ds(q * SC_LANES, SC_LANES)] = (
                        inv_v[pl.ds(j * SC_ROWS + q * SC_LANES, SC_LANES)])
            for slot in range(2):
                read(2 * jj + slot, slot).wait()
                write(slot).start()
            for slot in range(2):
                write(slot).wait()

    return k(cls, rank, pstart, table)


def _sc_unsort(table, inv):
    t = inv.shape[0]
    width = table.shape[1]
    tw = t // SC_WORKERS
    n_chunks = tw // SC_ROWS
    assert t % (SC_WORKERS * 2 * SC_ROWS) == 0
    mesh = plsc.VectorSubcoreMesh(core_axis_name="c", subcore_axis_name="s")

    @functools.partial(
        pl.kernel, mesh=mesh, out_type=jax.ShapeDtypeStruct((t, width), I32),
        scratch_types=_sc_scratch(tw, width),
        compiler_params=pltpu.CompilerParams(needs_layout_passes=False),
        name="unsort")
    def k(table_hbm, inv_hbm, out_hbm, idx_all, idx_a, idx_b, rows_a, rows_b, sem_a, sem_b):
        base = _sc_worker_id() * tw
        pltpu.sync_copy(inv_hbm.at[pl.ds(base, tw)], idx_all)
        _sc_gather_rows(table_hbm, out_hbm, idx_all, base, n_chunks,
                        (idx_a, idx_b), (rows_a, rows_b), (sem_a, sem_b))

    return k(table, inv)


def _combine_kernel(x_ref, y_ref, mod_ref, gf_ref, o_ref, *, final):
    xn = x_ref[0] + mod_ref[0][5:6] * _unpack_pairs(y_ref[0])
    if final:
        xn = _rms(xn) * gf_ref[...]
    o_ref[0] = xn


def _combine(x, y, mod, g_final, final):
    b, seq, d = x.shape
    tm = TM_COMBINE
    blk = pl.BlockSpec((1, tm, d), lambda bi, i: (bi, i, 0))
    return pl.pallas_call(
        functools.partial(_combine_kernel, final=final),
        grid=(b, seq // tm),
        in_specs=[blk, pl.BlockSpec((1, tm, W_PACK), lambda bi, i: (bi, i, 0)),
                  pl.BlockSpec((1, N_MOD, d), lambda bi, i: (bi, 0, 0)),
                  pl.BlockSpec((1, d), lambda bi, i: (0, 0))],
        out_specs=blk,
        out_shape=jax.ShapeDtypeStruct((b, seq, d), F32),
        compiler_params=pltpu.CompilerParams(
            dimension_semantics=("arbitrary", "arbitrary"), vmem_limit_bytes=VMEM_LIMIT),
        name="combine",
    )(x, y, mod, g_final)


def _dft_tables(seq):
    n = np.arange(seq)
    ang = 2.0 * np.pi * ((n[:, None] * n[None, :]) % seq) / seq
    scale = 1.0 / np.sqrt(seq)
    return (np.cos(ang) * scale).astype(np.float32), (-np.sin(ang) * scale).astype(np.float32)


def _dft64_blockdiag():
    n = np.arange(HEAD_DIM)
    ang = 2.0 * np.pi * ((n[:, None] * n[None, :]) % HEAD_DIM) / HEAD_DIM
    c, s = np.cos(ang) / np.sqrt(HEAD_DIM), np.sin(ang) / np.sqrt(HEAD_DIM)
    bd = np.zeros((GROUP, 2 * GROUP), np.float32)
    for h in range(N_HEADS):
        r = slice(h * HEAD_DIM, (h + 1) * HEAD_DIM)
        bd[r, h * HEAD_DIM:(h + 1) * HEAD_DIM] = c
        bd[r, GROUP + h * HEAD_DIM:GROUP + (h + 1) * HEAD_DIM] = s
    return bd


def _block_diag(w):
    n, k, _ = w.shape
    eye = jnp.eye(n, dtype=w.dtype)
    return (eye[:, None, :, None] * w[:, :, None, :]).reshape(n * k, n * k)


def _route_tables(counts, t, tm):
    nt = t // tm + N_CLASSES
    nt += -nt % 8
    ntile_c = (counts + tm - 1) // tm
    tile_end = jnp.cumsum(ntile_c)
    pstart = (tile_end - ntile_c) * tm
    n_used = tile_end[-1]
    tile_ids = jnp.minimum(jnp.arange(nt, dtype=jnp.int32), n_used - 1)
    tile_cls = jnp.sum((tile_ids[:, None] >= tile_end[None, :]).astype(jnp.int32), axis=1)
    pair_lo = jnp.array([p[0] for p in PAIRS], jnp.int32)
    pair_hi = jnp.array([p[1] for p in PAIRS], jnp.int32)
    grp, pid = tile_cls // len(PAIRS), tile_cls % len(PAIRS)
    tile_ea = grp * EXPERTS_PER_GROUP + jnp.take(pair_lo, pid)
    tile_eb = grp * EXPERTS_PER_GROUP + jnp.take(pair_hi, pid)
    pstart = jnp.concatenate([pstart, jnp.zeros((CLS_PAD - N_CLASSES,), I32)])
    return pstart, tile_ea, tile_eb, n_used.reshape(1).astype(jnp.int32), nt


def _trunk(x, mods, lw, shared):
    b, seq, d = x.shape
    t = b * seq
    depth = len(lw)
    prev = None
    for l in range(depth):
        w = lw[l]
        mod = mods[l]
        xg, part = _mix_in(x, prev, mod, w["g1"], w["w_in"], shared["d64"], w["ws"], w["bs"],
                           w["wc"], w["wp"], w["ps"], w["gg"], w["wo_bcd"])
        x_mid, h2, cls, rank, cnt = _mix_out(
            shared["cs"], shared["ss"], xg, part, x, prev, mod, w["wf"], w["gg"], w["wo_a"],
            w["g2"], shared["wr_hi"], shared["wr_lo"], shared["b_router"], shared["tri"])
        counts = cnt[:N_CLASSES, 0].astype(I32)
        pstart, tile_ea, tile_eb, n_used, nt = _route_tables(counts, t, TM_MOE)
        xs, inv = _sc_dispatch(cls.reshape(t), rank.reshape(t), pstart, h2.reshape(t, W_ROW),
                               nt * TM_MOE)
        ys = _moe_ffn(tile_ea + l * N_EXPERTS, tile_eb + l * N_EXPERTS, n_used, xs,
                      shared["wg"], shared["wu"], shared["wd"])
        y_tok = _sc_unsort(ys, inv).reshape(b, seq, W_PACK)
        x, prev = x_mid, (y_tok, mod)
    return _combine(x, prev[0], prev[1], shared["g_final"], final=True)


def kernel(x_prompt, x_sample, c_prompt, c_sample, w_ada, b_ada, g_norm1, w_in, w_fourier, w_spatial, b_spatial, w_conv, w_pool, pool_scale, g_group, w_out, g_norm2, w_router, b_router, w_exp_gate, w_exp_up, w_exp_down, g_final):
    depth = w_in.shape[0]
    seq = x_prompt.shape[1]
    d = D_MODEL
    nb_p = c_prompt.shape[0]

    cs, ss = _dft_tables(seq)
    wr_hi = w_router.astype(BF16)
    wr_lo = (w_router - wr_hi.astype(F32)).astype(BF16)
    shared = {
        "cs": jnp.asarray(cs).astype(BF16),
        "ss": jnp.asarray(ss).astype(BF16),
        "d64": jnp.asarray(_dft64_blockdiag()).astype(BF16),
        "wr_hi": wr_hi.T, "wr_lo": wr_lo.T,
        "b_router": b_router.reshape(N_EXPERTS, 1).astype(F32),
        "g_final": g_final.reshape(1, d),
        "tri": jnp.asarray(np.triu(np.ones((TM_MIX, TM_MIX), np.float32))).astype(BF16),
    }
    lw = []
    for l in range(depth):
        lw.append({
            "g1": g_norm1[l].reshape(1, d),
            "w_in": w_in[l].astype(BF16),
            "ws": w_spatial[l].reshape(N_HEADS * CHUNK, CHUNK).astype(BF16),
            "bs": jnp.repeat(b_spatial[l].T, HEAD_DIM, axis=1),
            "wc": w_conv[l],
            "wp": _block_diag(w_pool[l]).astype(BF16),
            "ps": pool_scale[l].reshape(1, GROUP),
            "gg": g_group[l].reshape(1, d),
            "wo_bcd": w_out[l, GROUP:].astype(BF16),
            "wo_a": w_out[l, :GROUP].astype(BF16),
            "wf": _block_diag(w_fourier[l]).astype(BF16),
            "g2": g_norm2[l].reshape(1, d),
        })
    shared["wg"] = w_exp_gate.astype(BF16).reshape(depth * N_EXPERTS, d, D_EXPERT)
    shared["wu"] = w_exp_up.astype(BF16).reshape(depth * N_EXPERTS, d, D_EXPERT)
    shared["wd"] = w_exp_down.astype(BF16).reshape(depth * N_EXPERTS, D_EXPERT, d)

    c_all = jnp.concatenate([c_prompt, c_sample], axis=0)
    mod_all = _ada(c_all, w_ada, b_ada).reshape(depth, c_all.shape[0], N_MOD, d)
    mods_p = [mod_all[l, :nb_p] for l in range(depth)]
    mods_s = [mod_all[l, nb_p:] for l in range(depth)]
    y_prompt = _trunk(x_prompt, mods_p, lw, shared)
    y_sample = _trunk(x_sample, mods_s, lw, shared)
    return (y_prompt, y_sample)
```

```python
import functools

import numpy as np
import jax
import jax.numpy as jnp
from jax import lax
from jax.experimental import pallas as pl
from jax.experimental.pallas import tpu as pltpu
from jax.experimental.pallas import tpu_sc as plsc

F32 = jnp.float32
BF16 = jnp.bfloat16
I32 = jnp.int32
U32 = jnp.uint32

D_MODEL = 1024
GROUP = 256
N_HEADS = 4
HEAD_DIM = 64
CHUNK = 128
D_IN = 7 * GROUP
N_EXPERTS = 16
N_EGROUPS = 4
EXPERTS_PER_GROUP = 4
D_EXPERT = 512
N_MOD = 6
EPS = 1e-6
HALO = 8
PAIRS = ((0, 1), (0, 2), (0, 3), (1, 2), (1, 3), (2, 3))
N_CLASSES = N_EGROUPS * len(PAIRS)
CLS_PAD = 32

TM_IN = 512
TM_MIX = 512
TM_MOE = 512
SEQ_PER_STEP = 2
TM_COMBINE = 1024
VMEM_LIMIT = 56 * 1024 * 1024

W_PACK = D_MODEL // 2
W_EXTRA = 128
W_ROW = W_PACK + W_EXTRA

SC_CORES = 2
SC_SUBCORES = 16
SC_LANES = 16
SC_WORKERS = SC_CORES * SC_SUBCORES
SC_ROWS = 64
SC_SCAN = 2048


def _rms(x):
    return x * lax.rsqrt(jnp.mean(x * x, axis=-1, keepdims=True) + EPS)


def _dot(a, b):
    return jnp.dot(a, b, preferred_element_type=F32)


def _pack_pairs(x):
    k = x.shape[1] // 2
    xb = x.astype(BF16).astype(F32)
    hi = lax.bitcast_convert_type(xb[:, :k], U32)
    lo = lax.bitcast_convert_type(xb[:, k:], U32)
    return lax.bitcast_convert_type(hi | (lo >> 16), I32)


def _unpack_pairs(w):
    u = lax.bitcast_convert_type(w, U32)
    hi = lax.bitcast_convert_type(u & jnp.uint32(0xFFFF0000), F32)
    lo = lax.bitcast_convert_type(u << 16, F32)
    return jnp.concatenate([hi, lo], axis=1)


def _ada_kernel(c_ref, w_ref, b_ref, o_ref):
    c = c_ref[...]
    sc = c * jax.nn.sigmoid(c)
    o_ref[0] = _dot(sc.astype(BF16), w_ref[0].astype(BF16)) + b_ref[0]


def _ada(c, w_ada, b_ada):
    depth, d, n = w_ada.shape
    bc = c.shape[0]
    tn = 1536
    return pl.pallas_call(
        _ada_kernel,
        grid=(depth, n // tn),
        in_specs=[
            pl.BlockSpec((bc, d), lambda l, j: (0, 0)),
            pl.BlockSpec((1, d, tn), lambda l, j: (l, 0, j)),
            pl.BlockSpec((1, 1, tn), lambda l, j: (l, 0, j)),
        ],
        out_specs=pl.BlockSpec((1, bc, tn), lambda l, j: (l, 0, j)),
        out_shape=jax.ShapeDtypeStruct((depth, bc, n), F32),
        compiler_params=pltpu.CompilerParams(
            dimension_semantics=("arbitrary", "arbitrary"), vmem_limit_bytes=VMEM_LIMIT),
        name="ada_mod",
    )(c, w_ada, b_ada.reshape(depth, 1, n))


def _mix_in_kernel(*refs, tm, seq, fused):
    if fused:
        ym_ref, yp_ref, yn_ref, modp_ref = refs[3:7]
        refs = refs[:3] + refs[7:]
    (xm_ref, xp_ref, xn_ref, mod_ref, g1_ref, win_ref, d64_ref, ws_ref, bs_ref,
     wc_ref, wp_ref, ps_ref, gg_ref, wo_ref, xg_ref, part_ref) = refs
    i = pl.program_id(1)
    ne = tm + 2 * HALO
    xe = jnp.concatenate([xp_ref[0], xm_ref[0], xn_ref[0]], axis=0)
    if fused:
        ye = jnp.concatenate([yp_ref[0], ym_ref[0], yn_ref[0]], axis=0)
        xe = xe + modp_ref[0][5:6] * _unpack_pairs(ye)
    mod = mod_ref[0]
    h = _rms(xe) * g1_ref[...] * (1.0 + mod[1:2]) + mod[0:1]
    proj = _dot(h.astype(BF16), win_ref[...])

    gpos = lax.broadcasted_iota(jnp.int32, (ne, GROUP), 0) + (i * tm - HALO)
    valid = (gpos >= 0) & (gpos < seq)
    pm = proj[HALO:HALO + tm]

    xg_ref[0] = _dot(pm[:, 0:GROUP].astype(BF16), d64_ref[...]).astype(BF16)

    u = pm[:, GROUP:2 * GROUP]
    v = pm[:, 2 * GROUP:3 * GROUP]
    head = lax.broadcasted_iota(jnp.int32, (CHUNK, GROUP), 1) // HEAD_DIM
    yb_chunks = []
    for c in range(tm // CHUNK):
        vc = v[c * CHUNK:(c + 1) * CHUNK].astype(BF16)
        m_all = _dot(ws_ref[...], vc)
        mixed = bs_ref[...]
        for hh in range(N_HEADS):
            mixed = mixed + jnp.where(head == hh, m_all[hh * CHUNK:(hh + 1) * CHUNK], 0.0)
        yb_chunks.append(u[c * CHUNK:(c + 1) * CHUNK] * mixed)
    yb = jnp.concatenate(yb_chunks, axis=0)

    z = jnp.where(valid, proj[:, 4 * GROUP:5 * GROUP] * proj[:, 5 * GROUP:6 * GROUP], 0.0)
    conv = (pltpu.roll(z, 1, 0)[HALO:HALO + tm] * wc_ref[0:1, :]
            + z[HALO:HALO + tm] * wc_ref[1:2, :]
            + pltpu.roll(z, ne - 1, 0)[HALO:HALO + tm] * wc_ref[2:3, :])
    yc = pm[:, 3 * GROUP:4 * GROUP] * conv

    p = jnp.where(valid, proj[:, 6 * GROUP:7 * GROUP], 0.0)
    a2 = p + pltpu.roll(p, 1, 0)
    a4 = a2 + pltpu.roll(a2, 2, 0)
    a8 = a4 + pltpu.roll(a4, 4, 0)
    a16 = a8 + pltpu.roll(a8, 8, 0)
    w2 = a2[HALO:HALO + tm]
    w4 = pltpu.roll(a4, ne - 1, 0)[HALO:HALO + tm]
    w8 = pltpu.roll(a8, ne - 3, 0)[HALO:HALO + tm]
    w16 = pltpu.roll(a16, ne - 7, 0)[HALO:HALO + tm]
    grp = lax.broadcasted_iota(jnp.int32, (tm, GROUP), 1) // HEAD_DIM
    wsum = jnp.where(grp == 0, w2, jnp.where(grp == 1, w4, jnp.where(grp == 2, w8, w16)))
    left = jnp.left_shift(1, grp)
    t = lax.broadcasted_iota(jnp.int32, (tm, GROUP), 0) + i * tm
    cnt = jnp.minimum(t + left, seq) - jnp.maximum(t - left, 0)
    pooled = wsum / cnt.astype(F32) - p[HALO:HALO + tm]
    yd = _dot(pooled.astype(BF16), wp_ref[...]) * ps_ref[...]

    gg = gg_ref[...]
    ycat = jnp.concatenate([
        (_rms(yb) * gg[:, GROUP:2 * GROUP]).astype(BF16),
        (_rms(yc) * gg[:, 2 * GROUP:3 * GROUP]).astype(BF16),
        (_rms(yd) * gg[:, 3 * GROUP:4 * GROUP]).astype(BF16)], axis=1)
    part_ref[0] = _dot(ycat, wo_ref[...]).astype(part_ref.dtype)


def _mix_in(x, prev, mod, g1, w_in, d64, ws, bs, wc, wp, ps, gg, wo_bcd):
    b, seq, d = x.shape
    tm = TM_IN
    nt = seq // tm
    hb = tm // HALO
    full = lambda *shape: pl.BlockSpec(shape, lambda bi, i: (0,) * len(shape))
    mod_spec = pl.BlockSpec((1, N_MOD, d), lambda bi, i: (bi, 0, 0))

    def rows(width):
        return [
            pl.BlockSpec((1, tm, width), lambda bi, i: (bi, i, 0)),
            pl.BlockSpec((1, HALO, width), lambda bi, i: (bi, jnp.maximum(i * hb - 1, 0), 0)),
            pl.BlockSpec((1, HALO, width),
                         lambda bi, i: (bi, jnp.minimum((i + 1) * hb, seq // HALO - 1), 0)),
        ]

    fused = prev is not None
    prev_specs = rows(W_PACK) + [mod_spec] if fused else []
    prev_args = (prev[0], prev[0], prev[0], prev[1]) if fused else ()
    return pl.pallas_call(
        functools.partial(_mix_in_kernel, tm=tm, seq=seq, fused=fused),
        grid=(b, nt),
        in_specs=rows(d) + prev_specs + [
            mod_spec,
            full(1, d), full(d, D_IN), full(GROUP, 2 * GROUP), full(N_HEADS * CHUNK, CHUNK),
            full(CHUNK, GROUP), full(3, GROUP), full(GROUP, GROUP), full(1, GROUP),
            full(1, d), full(3 * GROUP, d),
        ],
        out_specs=[
            pl.BlockSpec((1, tm, 2 * GROUP), lambda bi, i: (bi, i, 0)),
            pl.BlockSpec((1, tm, d), lambda bi, i: (bi, i, 0)),
        ],
        out_shape=[
            jax.ShapeDtypeStruct((b, seq, 2 * GROUP), BF16),
            jax.ShapeDtypeStruct((b, seq, d), BF16),
        ],
        compiler_params=pltpu.CompilerParams(
            dimension_semantics=("arbitrary", "arbitrary"), vmem_limit_bytes=VMEM_LIMIT),
        name="mix_in",
    )(x, x, x, *prev_args, mod, g1, w_in, d64, ws, bs, wc, wp, ps, gg, wo_bcd)


def _top2_of4(vals, aux):
    best, bi, ba = vals[0], jnp.zeros_like(vals[0], jnp.int32), aux[0]
    for j in range(1, 4):
        gt = vals[j] > best
        best = jnp.where(gt, vals[j], best)
        bi = jnp.where(gt, j, bi)
        ba = jnp.where(gt, aux[j], ba)
    sec = jnp.full_like(best, -jnp.inf)
    si, sa = jnp.zeros_like(bi), aux[0]
    for j in range(4):
        cand = jnp.where(bi == j, -jnp.inf, vals[j])
        gt = cand > sec
        sec = jnp.where(gt, cand, sec)
        si = jnp.where(gt, j, si)
        sa = jnp.where(gt, aux[j], sa)
    return best, sec, bi, si, ba, sa


def _mix_out_kernel(*refs, fused):
    y_ref = modp_ref = None
    if fused:
        y_ref, modp_ref = refs[5:7]
        refs = refs[:5] + refs[7:]
    (cs_ref, ss_ref, xg_ref, part_ref, x_ref, mod_ref, wf_ref, gg_ref, wo_ref,
     g2_ref, wrh_ref, wrl_ref, br_ref, tri_ref, xo_ref, h2_ref, cls_ref, rk_ref,
     cnt_ref, carry_ref) = refs

    @pl.when((pl.program_id(0) == 0) & (pl.program_id(1) == 0))
    def _():
        carry_ref[...] = jnp.zeros_like(carry_ref)

    nq = xg_ref.shape[0]
    xc = jnp.concatenate([xg_ref[q, :, 0:GROUP] for q in range(nq)], axis=1)
    xs = jnp.concatenate([xg_ref[q, :, GROUP:2 * GROUP] for q in range(nq)], axis=1)
    f_all = _dot(cs_ref[...], xc) + _dot(ss_ref[...], xs)
    for q in range(nq):
        x_in = x_ref[q]
        if fused:
            x_in = x_in + modp_ref[q][5:6] * _unpack_pairs(y_ref[q])
        _mix_out_tokens(q, f_all[:, q * GROUP:(q + 1) * GROUP], x_in, part_ref, mod_ref, wf_ref,
                        gg_ref, wo_ref, g2_ref, wrh_ref, wrl_ref, br_ref, tri_ref, xo_ref, h2_ref,
                        cls_ref, rk_ref, carry_ref)
    cnt_ref[...] = carry_ref[...]


def _mix_out_tokens(q, f, x_in, part_ref, mod_ref, wf_ref, gg_ref, wo_ref, g2_ref, wrh_ref,
                    wrl_ref, br_ref, tri_ref, xo_ref, h2_ref, cls_ref, rk_ref, carry_ref):
    ya = _dot(f.astype(BF16), wf_ref[...])
    ya = _rms(ya) * gg_ref[:, 0:GROUP]
    mix = _dot(ya.astype(BF16), wo_ref[...]) + part_ref[q].astype(F32)
    mod = mod_ref[q]
    xn = x_in + mod[2:3] * mix
    xo_ref[q] = xn
    h2 = _rms(xn) * g2_ref[...] * (1.0 + mod[4:5]) + mod[3:4]
    h_hi = h2.astype(BF16)
    h2_ref[q, :, 0:W_PACK] = _pack_pairs(h2)
    h_lo = (h2 - h_hi.astype(F32)).astype(BF16)

    both = _dot(h_hi, wrh_ref[...])
    logits = both[:, 0:128] + both[:, 128:256] + _dot(h_lo, wrl_ref[...])
    lt = logits.T[0:N_EXPERTS]
    ex = jnp.exp(lt - jnp.max(lt, axis=0, keepdims=True))
    scores = ex / jnp.sum(ex, axis=0, keepdims=True)
    biased = scores + br_ref[...]

    best_gs = None
    for g in range(N_EGROUPS):
        rows = [biased[g * 4 + j:g * 4 + j + 1] for j in range(4)]
        srow = [scores[g * 4 + j:g * 4 + j + 1] for j in range(4)]
        t1, t2, i1, i2, s1, s2 = _top2_of4(rows, srow)
        gs = t1 + t2
        if best_gs is None:
            best_gs, sel = gs, jnp.zeros_like(i1)
            bi1, bi2, bs1, bs2 = i1, i2, s1, s2
        else:
            gt = gs > best_gs
            best_gs = jnp.where(gt, gs, best_gs)
            sel = jnp.where(gt, g, sel)
            bi1, bi2 = jnp.where(gt, i1, bi1), jnp.where(gt, i2, bi2)
            bs1, bs2 = jnp.where(gt, s1, bs1), jnp.where(gt, s2, bs2)
    first_lo = bi1 < bi2
    lo = jnp.where(first_lo, bi1, bi2)
    hi = jnp.where(first_lo, bi2, bi1)
    den = bs1 + bs2
    w_a = jnp.where(first_lo, bs1, bs2) / den
    w_b = jnp.where(first_lo, bs2, bs1) / den
    pair = jnp.where(lo == 0, hi - 1, jnp.where(lo == 1, hi + 1, 5))
    cls_i = sel * len(PAIRS) + pair
    tk = cls_i.shape[1]
    gate_cols = jnp.concatenate([w_a, w_b, jnp.zeros((W_EXTRA - 2, tk), F32)], axis=0).T
    h2_ref[q, :, W_PACK:W_ROW] = lax.bitcast_convert_type(gate_cols, I32)

    onehot = lax.broadcasted_iota(jnp.int32, (CLS_PAD, tk), 0) == cls_i
    prefix = _dot(jnp.where(onehot, 1.0, 0.0).astype(BF16), tri_ref[...])
    carry = carry_ref[...]
    rank = jnp.sum(jnp.where(onehot, prefix - 1.0 + carry[:, 0:1], 0.0), axis=0, keepdims=True)
    carry_ref[...] = carry + prefix[:, tk - 1:tk]
    cls_ref[q] = cls_i
    rk_ref[q] = rank.astype(I32)


def _mix_out(cs, ss, xg, part, x, prev, mod, wf_bd, gg, wo_a, g2, wr_hi, wr_lo, b_router, tri):
    b, seq, d = x.shape
    tk = TM_MIX
    nk = seq // tk
    nq = SEQ_PER_STEP
    full = lambda *shape: pl.BlockSpec(shape, lambda k, bi: (0,) * len(shape))
    rows = lambda width: pl.BlockSpec((nq, tk, width), lambda k, bi: (bi, k, 0))
    lanes = pl.BlockSpec((nq, 1, tk), lambda k, bi: (bi, 0, k))
    mod_spec = pl.BlockSpec((nq, N_MOD, d), lambda k, bi: (bi, 0, 0))
    fused = prev is not None
    prev_specs = [rows(W_PACK), mod_spec] if fused else []
    prev_args = tuple(prev) if fused else ()
    return pl.pallas_call(
        functools.partial(_mix_out_kernel, fused=fused),
        grid=(nk, b // nq),
        in_specs=[
            pl.BlockSpec((tk, seq), lambda k, bi: (k, 0)),
            pl.BlockSpec((tk, seq), lambda k, bi: (k, 0)),
            pl.BlockSpec((nq, seq, 2 * GROUP), lambda k, bi: (bi, 0, 0)),
            rows(d), rows(d),
        ] + prev_specs + [
            mod_spec,
            full(GROUP, GROUP), full(1, d), full(GROUP, d), full(1, d),
            full(d, 256), full(d, 128), full(N_EXPERTS, 1), full(tk, tk),
        ],
        out_specs=[rows(d), rows(W_ROW), lanes, lanes, full(CLS_PAD, 128)],
        out_shape=[
            jax.ShapeDtypeStruct((b, seq, d), F32),
            jax.ShapeDtypeStruct((b, seq, W_ROW), I32),
            jax.ShapeDtypeStruct((b, 1, seq), I32),
            jax.ShapeDtypeStruct((b, 1, seq), I32),
            jax.ShapeDtypeStruct((CLS_PAD, 128), F32),
        ],
        scratch_shapes=[pltpu.VMEM((CLS_PAD, 128), F32)],
        compiler_params=pltpu.CompilerParams(
            dimension_semantics=("arbitrary", "arbitrary"), vmem_limit_bytes=VMEM_LIMIT),
        name="mix_out",
    )(cs, ss, xg, part, x, *prev_args, mod, wf_bd, gg, wo_a, g2, wr_hi, wr_lo, b_router, tri)


def _moe_kernel(ea_ref, eb_ref, nu_ref, xs_ref, wga_ref, wua_ref, wda_ref,
                wgb_ref, wub_ref, wdb_ref, o_ref):
    @pl.when(pl.program_id(0) < nu_ref[0])
    def _():
        xs = _unpack_pairs(xs_ref[:, 0:W_PACK]).astype(BF16)
        wt = lax.bitcast_convert_type(xs_ref[:, W_PACK:W_ROW], F32)

        def ffn(wg, wu, wd):
            a = _dot(xs, wg[0])
            hid = (a * jax.nn.sigmoid(a)) * _dot(xs, wu[0])
            return _dot(hid.astype(BF16), wd[0])

        o_ref[...] = _pack_pairs(wt[:, 0:1] * ffn(wga_ref, wua_ref, wda_ref)
                                 + wt[:, 1:2] * ffn(wgb_ref, wub_ref, wdb_ref))


def _moe_ffn(tile_ea, tile_eb, n_used, xs, wg, wu, wd):
    p = xs.shape[0]
    d = D_MODEL
    tm = TM_MOE
    nt = p // tm
    row = lambda i, ea, eb, nu: (jnp.minimum(i, nu[0] - 1), 0)
    exp_a = lambda i, ea, eb, nu: (ea[i], 0, 0)
    exp_b = lambda i, ea, eb, nu: (eb[i], 0, 0)
    return pl.pallas_call(
        _moe_kernel,
        grid_spec=pltpu.PrefetchScalarGridSpec(
            num_scalar_prefetch=3,
            grid=(nt,),
            in_specs=[
                pl.BlockSpec((tm, W_ROW), row),
                pl.BlockSpec((1, d, D_EXPERT), exp_a),
                pl.BlockSpec((1, d, D_EXPERT), exp_a),
                pl.BlockSpec((1, D_EXPERT, d), exp_a),
                pl.BlockSpec((1, d, D_EXPERT), exp_b),
                pl.BlockSpec((1, d, D_EXPERT), exp_b),
                pl.BlockSpec((1, D_EXPERT, d), exp_b),
            ],
            out_specs=pl.BlockSpec((tm, W_PACK), row),
        ),
        out_shape=jax.ShapeDtypeStruct((p, W_PACK), I32),
        compiler_params=pltpu.CompilerParams(
            dimension_semantics=("arbitrary",), vmem_limit_bytes=VMEM_LIMIT),
        name="moe_ffn",
    )(tile_ea, tile_eb, n_used, xs, wg, wu, wd, wg, wu, wd)


def _sc_worker_id():
    return lax.axis_index("s") * SC_CORES + lax.axis_index("c")


def _sc_gather_rows(table_hbm, out_hbm, idx_all, out_base, n_chunks, idx_bufs, row_bufs, sems):
    def copy(slot):
        return pltpu.make_async_copy(table_hbm.at[idx_bufs[slot]], row_bufs[slot], sems[slot])

    def start(j, slot):
        for q in range(SC_ROWS // SC_LANES):
            idx_bufs[slot][pl.ds(q * SC_LANES, SC_LANES)] = (
                idx_all[pl.ds(j * SC_ROWS + q * SC_LANES, SC_LANES)])
        copy(slot).start()

    def flush(j, slot):
        copy(slot).wait()
        pltpu.sync_copy(row_bufs[slot], out_hbm.at[pl.ds(out_base + j * SC_ROWS, SC_ROWS)])

    start(0, 0)

    @pl.loop(0, n_chunks // 2)
    def _(jj):
        j = 2 * jj
        start(j + 1, 1)
        flush(j, 0)

        @pl.when(j + 2 < n_chunks)
        def _():
            start(j + 2, 0)

        flush(j + 1, 1)


def _sc_scratch(n_idx, width):
    return [
        pltpu.VMEM((n_idx,), I32),
        pltpu.VMEM((SC_ROWS,), I32), pltpu.VMEM((SC_ROWS,), I32),
        pltpu.VMEM((SC_ROWS, width), I32), pltpu.VMEM((SC_ROWS, width), I32),
        pltpu.SemaphoreType.DMA, pltpu.SemaphoreType.DMA,
    ]


def _sc_dispatch(cls, rank, pstart, table, p):
    t, width = table.shape
    tw = t // SC_WORKERS
    n_chunks = tw // SC_ROWS
    assert t % (SC_WORKERS * 2 * SC_ROWS) == 0
    mesh = plsc.VectorSubcoreMesh(core_axis_name="c", subcore_axis_name="s")

    @functools.partial(
        pl.kernel, mesh=mesh,
        out_type=[jax.ShapeDtypeStruct((p, width), I32), jax.ShapeDtypeStruct((t,), I32)],
        scratch_types=[pltpu.VMEM((tw,), I32), pltpu.VMEM((CLS_PAD,), I32)]
        + _sc_scratch(tw, width) + [pltpu.SemaphoreType.DMA, pltpu.SemaphoreType.DMA],
        compiler_params=pltpu.CompilerParams(needs_layout_passes=False),
        name="dispatch")
    def k(cls_hbm, rank_hbm, ps_hbm, table_hbm, xs_hbm, inv_hbm,
          cls_v, ps_v, inv_v, idx_a, idx_b, rows_a, rows_b, rsem_a, rsem_b, wsem_a, wsem_b):
        base = _sc_worker_id() * tw
        pltpu.sync_copy(ps_hbm, ps_v)
        pltpu.sync_copy(cls_hbm.at[pl.ds(base, tw)], cls_v)
        pltpu.sync_copy(rank_hbm.at[pl.ds(base, tw)], inv_v)

        @pl.loop(0, tw // SC_LANES)
        def _(i):
            sl = pl.ds(i * SC_LANES, SC_LANES)
            inv_v[sl] = plsc.load_gather(ps_v, [cls_v[sl]]) + inv_v[sl]

        pltpu.sync_copy(inv_v, inv_hbm.at[pl.ds(base, tw)])

        slots = ((idx_a, rows_a, rsem_a, wsem_a), (idx_b, rows_b, rsem_b, wsem_b))

        def read(j, slot):
            _, rows, rsem, _ = slots[slot]
            return pltpu.make_async_copy(
                table_hbm.at[pl.ds(base + j * SC_ROWS, SC_ROWS)], rows, rsem)

        def write(slot):
            idx, rows, _, wsem = slots[slot]
            return pltpu.make_async_copy(rows, xs_hbm.at[idx], wsem)

        @pl.loop(0, n_chunks // 2)
        def _(jj):
            for slot in range(2):
                j = 2 * jj + slot
                read(j, slot).start()
                for q in range(SC_ROWS // SC_LANES):
                    slots[slot][0][pl.ds(q * SC_LANES, SC_LANES)] = (
                        inv_v[pl.ds(j * SC_ROWS + q * SC_LANES, SC_LANES)])
            for slot in range(2):
                read(2 * jj + slot, slot).wait()
                write(slot).start()
            for slot in range(2):
                write(slot).wait()

    return k(cls, rank, pstart, table)


def _sc_unsort(table, inv):
    t = inv.shape[0]
    width = table.shape[1]
    tw = t // SC_WORKERS
    n_chunks = tw // SC_ROWS
    assert t % (SC_WORKERS * 2 * SC_ROWS) == 0
    mesh = plsc.VectorSubcoreMesh(core_axis_name="c", subcore_axis_name="s")

    @functools.partial(
        pl.kernel, mesh=mesh, out_type=jax.ShapeDtypeStruct((t, width), I32),
        scratch_types=_sc_scratch(tw, width),
        compiler_params=pltpu.CompilerParams(needs_layout_passes=False),
        name="unsort")
    def k(table_hbm, inv_hbm, out_hbm, idx_all, idx_a, idx_b, rows_a, rows_b, sem_a, sem_b):
        base = _sc_worker_id() * tw
        pltpu.sync_copy(inv_hbm.at[pl.ds(base, tw)], idx_all)
        _sc_gather_rows(table_hbm, out_hbm, idx_all, base, n_chunks,
                        (idx_a, idx_b), (rows_a, rows_b), (sem_a, sem_b))

    return k(table, inv)


def _combine_kernel(x_ref, y_ref, mod_ref, gf_ref, o_ref, *, final):
    xn = x_ref[0] + mod_ref[0][5:6] * _unpack_pairs(y_ref[0])
    if final:
        xn = _rms(xn) * gf_ref[...]
    o_ref[0] = xn


def _combine(x, y, mod, g_final, final):
    b, seq, d = x.shape
    tm = TM_COMBINE
    blk = pl.BlockSpec((1, tm, d), lambda bi, i: (bi, i, 0))
    return pl.pallas_call(
        functools.partial(_combine_kernel, final=final),
        grid=(b, seq // tm),
        in_specs=[blk, pl.BlockSpec((1, tm, W_PACK), lambda bi, i: (bi, i, 0)),
                  pl.BlockSpec((1, N_MOD, d), lambda bi, i: (bi, 0, 0)),
                  pl.BlockSpec((1, d), lambda bi, i: (0, 0))],
        out_specs=blk,
        out_shape=jax.ShapeDtypeStruct((b, seq, d), F32),
        compiler_params=pltpu.CompilerParams(
            dimension_semantics=("arbitrary", "arbitrary"), vmem_limit_bytes=VMEM_LIMIT),
        name="combine",
    )(x, y, mod, g_final)


def _dft_tables(seq):
    n = np.arange(seq)
    ang = 2.0 * np.pi * ((n[:, None] * n[None, :]) % seq) / seq
    scale = 1.0 / np.sqrt(seq)
    return (np.cos(ang) * scale).astype(np.float32), (-np.sin(ang) * scale).astype(np.float32)


def _dft64_blockdiag():
    n = np.arange(HEAD_DIM)
    ang = 2.0 * np.pi * ((n[:, None] * n[None, :]) % HEAD_DIM) / HEAD_DIM
    c, s = np.cos(ang) / np.sqrt(HEAD_DIM), np.sin(ang) / np.sqrt(HEAD_DIM)
    bd = np.zeros((GROUP, 2 * GROUP), np.float32)
    for h in range(N_HEADS):
        r = slice(h * HEAD_DIM, (h + 1) * HEAD_DIM)
        bd[r, h * HEAD_DIM:(h + 1) * HEAD_DIM] = c
        bd[r, GROUP + h * HEAD_DIM:GROUP + (h + 1) * HEAD_DIM] = s
    return bd


def _block_diag(w):
    n, k, _ = w.shape
    eye = jnp.eye(n, dtype=w.dtype)
    return (eye[:, None, :, None] * w[:, :, None, :]).reshape(n * k, n * k)


def _route_tables(counts, t, tm):
    nt = t // tm + N_CLASSES
    nt += -nt % 8
    ntile_c = (counts + tm - 1) // tm
    tile_end = jnp.cumsum(ntile_c)
    pstart = (tile_end - ntile_c) * tm
    n_used = tile_end[-1]
    tile_ids = jnp.minimum(jnp.arange(nt, dtype=jnp.int32), n_used - 1)
    tile_cls = jnp.sum((tile_ids[:, None] >= tile_end[None, :]).astype(jnp.int32), axis=1)
    pair_lo = jnp.array([p[0] for p in PAIRS], jnp.int32)
    pair_hi = jnp.array([p[1] for p in PAIRS], jnp.int32)
    grp, pid = tile_cls // len(PAIRS), tile_cls % len(PAIRS)
    tile_ea = grp * EXPERTS_PER_GROUP + jnp.take(pair_lo, pid)
    tile_eb = grp * EXPERTS_PER_GROUP + jnp.take(pair_hi, pid)
    pstart = jnp.concatenate([pstart, jnp.zeros((CLS_PAD - N_CLASSES,), I32)])
    return pstart, tile_ea, tile_eb, n_used.reshape(1).astype(jnp.int32), nt


def _trunk(x, mods, lw, shared):
    b, seq, d = x.shape
    t = b * seq
    depth = len(lw)
    prev = None
    for l in range(depth):
        w = lw[l]
        mod = mods[l]
        xg, part = _mix_in(x, prev, mod, w["g1"], w["w_in"], shared["d64"], w["ws"], w["bs"],
                           w["wc"], w["wp"], w["ps"], w["gg"], w["wo_bcd"])
        x_mid, h2, cls, rank, cnt = _mix_out(
            shared["cs"], shared["ss"], xg, part, x, prev, mod, w["wf"], w["gg"], w["wo_a"],
            w["g2"], shared["wr_both"], shared["wr_hi"], shared["b_router"], shared["tri"])
        counts = cnt[:N_CLASSES, 0].astype(I32)
        pstart, tile_ea, tile_eb, n_used, nt = _route_tables(counts, t, TM_MOE)
        xs, inv = _sc_dispatch(cls.reshape(t), rank.reshape(t), pstart, h2.reshape(t, W_ROW),
                               nt * TM_MOE)
        ys = _moe_ffn(tile_ea + l * N_EXPERTS, tile_eb + l * N_EXPERTS, n_used, xs,
                      shared["wg"], shared["wu"], shared["wd"])
        y_tok = _sc_unsort(ys, inv).reshape(b, seq, W_PACK)
        x, prev = x_mid, (y_tok, mod)
    return _combine(x, prev[0], prev[1], shared["g_final"], final=True)


def kernel(x_prompt, x_sample, c_prompt, c_sample, w_ada, b_ada, g_norm1, w_in, w_fourier, w_spatial, b_spatial, w_conv, w_pool, pool_scale, g_group, w_out, g_norm2, w_router, b_router, w_exp_gate, w_exp_up, w_exp_down, g_final):
    depth = w_in.shape[0]
    seq = x_prompt.shape[1]
    d = D_MODEL
    nb_p = c_prompt.shape[0]

    cs, ss = _dft_tables(seq)
    w_router = jnp.pad(w_router, ((0, 0), (0, 128 - N_EXPERTS)))
    wr_hi = w_router.astype(BF16)
    wr_lo = (w_router - wr_hi.astype(F32)).astype(BF16)
    shared = {
        "cs": jnp.asarray(cs).astype(BF16),
        "ss": jnp.asarray(ss).astype(BF16),
        "d64": jnp.asarray(_dft64_blockdiag()).astype(BF16),
        "wr_both": jnp.concatenate([wr_hi, wr_lo], axis=1), "wr_hi": wr_hi,
        "b_router": b_router.reshape(N_EXPERTS, 1).astype(F32),
        "g_final": g_final.reshape(1, d),
        "tri": jnp.asarray(np.triu(np.ones((TM_MIX, TM_MIX), np.float32))).astype(BF16),
    }
    lw = []
    for l in range(depth):
        lw.append({
            "g1": g_norm1[l].reshape(1, d),
            "w_in": w_in[l].astype(BF16),
            "ws": w_spatial[l].reshape(N_HEADS * CHUNK, CHUNK).astype(BF16),
            "bs": jnp.repeat(b_spatial[l].T, HEAD_DIM, axis=1),
            "wc": w_conv[l],
            "wp": _block_diag(w_pool[l]).astype(BF16),
            "ps": pool_scale[l].reshape(1, GROUP),
            "gg": g_group[l].reshape(1, d),
            "wo_bcd": w_out[l, GROUP:].astype(BF16),
            "wo_a": w_out[l, :GROUP].astype(BF16),
            "wf": _block_diag(w_fourier[l]).astype(BF16),
            "g2": g_norm2[l].reshape(1, d),
        })
    shared["wg"] = w_exp_gate.astype(BF16).reshape(depth * N_EXPERTS, d, D_EXPERT)
    shared["wu"] = w_exp_up.astype(BF16).reshape(depth * N_EXPERTS, d, D_EXPERT)
    shared["wd"] = w_exp_down.astype(BF16).reshape(depth * N_EXPERTS, D_EXPERT, d)

    c_all = jnp.concatenate([c_prompt, c_sample], axis=0)
    mod_all = _ada(c_all, w_ada, b_ada).reshape(depth, c_all.shape[0], N_MOD, d)
    mods_p = [mod_all[l, :nb_p] for l in range(depth)]
    mods_s = [mod_all[l, nb_p:] for l in range(depth)]
    y_prompt = _trunk(x_prompt, mods_p, lw, shared)
    y_sample = _trunk(x_sample, mods_s, lw, shared)
    return (y_prompt, y_sample)
```

```python
import functools

import numpy as np
import jax
import jax.numpy as jnp
from jax import lax
from jax.experimental import pallas as pl
from jax.experimental.pallas import tpu as pltpu
from jax.experimental.pallas import tpu_sc as plsc

F32 = jnp.float32
BF16 = jnp.bfloat16
I32 = jnp.int32
U32 = jnp.uint32

D_MODEL = 1024
GROUP = 256
N_HEADS = 4
HEAD_DIM = 64
CHUNK = 128
D_IN = 7 * GROUP
N_EXPERTS = 16
N_EGROUPS = 4
EXPERTS_PER_GROUP = 4
D_EXPERT = 512
N_MOD = 6
EPS = 1e-6
HALO = 8
PAIRS = ((0, 1), (0, 2), (0, 3), (1, 2), (1, 3), (2, 3))
N_CLASSES = N_EGROUPS * len(PAIRS)
CLS_PAD = 32

TM_IN = 512
TM_MIX = 512
TM_MOE = 512
SEQ_PER_STEP = 2
TM_COMBINE = 1024
VMEM_LIMIT = 56 * 1024 * 1024

W_PACK = D_MODEL // 2
W_EXTRA = 128
W_ROW = W_PACK + W_EXTRA

SC_CORES = 2
SC_SUBCORES = 16
SC_LANES = 16
SC_WORKERS = SC_CORES * SC_SUBCORES
SC_ROWS = 64
SC_SCAN = 2048


def _rms(x):
    return x * lax.rsqrt(jnp.mean(x * x, axis=-1, keepdims=True) + EPS)


def _dot(a, b):
    return jnp.dot(a, b, preferred_element_type=F32)


def _pack_pairs(x):
    k = x.shape[1] // 2
    xb = x.astype(BF16).astype(F32)
    hi = lax.bitcast_convert_type(xb[:, :k], U32)
    lo = lax.bitcast_convert_type(xb[:, k:], U32)
    return lax.bitcast_convert_type(hi | (lo >> 16), I32)


def _unpack_pairs(w):
    u = lax.bitcast_convert_type(w, U32)
    hi = lax.bitcast_convert_type(u & jnp.uint32(0xFFFF0000), F32)
    lo = lax.bitcast_convert_type(u << 16, F32)
    return jnp.concatenate([hi, lo], axis=1)


def _ada_kernel(c_ref, w_ref, b_ref, o_ref):
    c = c_ref[...]
    sc = c * jax.nn.sigmoid(c)
    o_ref[0] = _dot(sc.astype(BF16), w_ref[0].astype(BF16)) + b_ref[0]


def _ada(c, w_ada, b_ada):
    depth, d, n = w_ada.shape
    bc = c.shape[0]
    tn = 1536
    return pl.pallas_call(
        _ada_kernel,
        grid=(depth, n // tn),
        in_specs=[
            pl.BlockSpec((bc, d), lambda l, j: (0, 0)),
            pl.BlockSpec((1, d, tn), lambda l, j: (l, 0, j)),
            pl.BlockSpec((1, 1, tn), lambda l, j: (l, 0, j)),
        ],
        out_specs=pl.BlockSpec((1, bc, tn), lambda l, j: (l, 0, j)),
        out_shape=jax.ShapeDtypeStruct((depth, bc, n), F32),
        compiler_params=pltpu.CompilerParams(
            dimension_semantics=("arbitrary", "arbitrary"), vmem_limit_bytes=VMEM_LIMIT),
        name="ada_mod",
    )(c, w_ada, b_ada.reshape(depth, 1, n))


def _mix_in_kernel(*refs, tm, seq, fused):
    if fused:
        ym_ref, yp_ref, yn_ref, modp_ref = refs[3:7]
        refs = refs[:3] + refs[7:]
    (xm_ref, xp_ref, xn_ref, mod_ref, g1_ref, win_ref, d64_ref, ws_ref, bs_ref,
     wc_ref, wp_ref, ps_ref, gg_ref, wo_ref, xg_ref, part_ref) = refs
    i = pl.program_id(1)
    ne = tm + 2 * HALO
    xe = jnp.concatenate([xp_ref[0], xm_ref[0], xn_ref[0]], axis=0)
    if fused:
        ye = jnp.concatenate([yp_ref[0], ym_ref[0], yn_ref[0]], axis=0)
        xe = xe + modp_ref[0][5:6] * _unpack_pairs(ye)
    mod = mod_ref[0]
    h = _rms(xe) * g1_ref[...] * (1.0 + mod[1:2]) + mod[0:1]
    proj = _dot(h.astype(BF16), win_ref[...])

    gpos = lax.broadcasted_iota(jnp.int32, (ne, GROUP), 0) + (i * tm - HALO)
    valid = (gpos >= 0) & (gpos < seq)
    pm = proj[HALO:HALO + tm]

    xg_ref[0] = _dot(pm[:, 0:GROUP].astype(BF16), d64_ref[...]).astype(BF16)

    u = pm[:, GROUP:2 * GROUP]
    v = pm[:, 2 * GROUP:3 * GROUP]
    head = lax.broadcasted_iota(jnp.int32, (CHUNK, GROUP), 1) // HEAD_DIM
    yb_chunks = []
    for c in range(tm // CHUNK):
        vc = v[c * CHUNK:(c + 1) * CHUNK].astype(BF16)
        m_all = _dot(ws_ref[...], vc)
        mixed = bs_ref[...]
        for hh in range(N_HEADS):
            mixed = mixed + jnp.where(head == hh, m_all[hh * CHUNK:(hh + 1) * CHUNK], 0.0)
        yb_chunks.append(u[c * CHUNK:(c + 1) * CHUNK] * mixed)
    yb = jnp.concatenate(yb_chunks, axis=0)

    z = jnp.where(valid, proj[:, 4 * GROUP:5 * GROUP] * proj[:, 5 * GROUP:6 * GROUP], 0.0)
    conv = (pltpu.roll(z, 1, 0)[HALO:HALO + tm] * wc_ref[0:1, :]
            + z[HALO:HALO + tm] * wc_ref[1:2, :]
            + pltpu.roll(z, ne - 1, 0)[HALO:HALO + tm] * wc_ref[2:3, :])
    yc = pm[:, 3 * GROUP:4 * GROUP] * conv

    p = jnp.where(valid, proj[:, 6 * GROUP:7 * GROUP], 0.0)
    a2 = p + pltpu.roll(p, 1, 0)
    a4 = a2 + pltpu.roll(a2, 2, 0)
    a8 = a4 + pltpu.roll(a4, 4, 0)
    a16 = a8 + pltpu.roll(a8, 8, 0)
    w2 = a2[HALO:HALO + tm]
    w4 = pltpu.roll(a4, ne - 1, 0)[HALO:HALO + tm]
    w8 = pltpu.roll(a8, ne - 3, 0)[HALO:HALO + tm]
    w16 = pltpu.roll(a16, ne - 7, 0)[HALO:HALO + tm]
    grp = lax.broadcasted_iota(jnp.int32, (tm, GROUP), 1) // HEAD_DIM
    wsum = jnp.where(grp == 0, w2, jnp.where(grp == 1, w4, jnp.where(grp == 2, w8, w16)))
    left = jnp.left_shift(1, grp)
    t = lax.broadcasted_iota(jnp.int32, (tm, GROUP), 0) + i * tm
    cnt = jnp.minimum(t + left, seq) - jnp.maximum(t - left, 0)
    pooled = wsum / cnt.astype(F32) - p[HALO:HALO + tm]
    yd = _dot(pooled.astype(BF16), wp_ref[...]) * ps_ref[...]

    gg = gg_ref[...]
    ycat = jnp.concatenate([
        (_rms(yb) * gg[:, GROUP:2 * GROUP]).astype(BF16),
        (_rms(yc) * gg[:, 2 * GROUP:3 * GROUP]).astype(BF16),
        (_rms(yd) * gg[:, 3 * GROUP:4 * GROUP]).astype(BF16)], axis=1)
    part_ref[0] = _dot(ycat, wo_ref[...]).astype(part_ref.dtype)


def _mix_in(x, prev, mod, g1, w_in, d64, ws, bs, wc, wp, ps, gg, wo_bcd):
    b, seq, d = x.shape
    tm = TM_IN
    nt = seq // tm
    hb = tm // HALO
    full = lambda *shape: pl.BlockSpec(shape, lambda bi, i: (0,) * len(shape))
    mod_spec = pl.BlockSpec((1, N_MOD, d), lambda bi, i: (bi, 0, 0))

    def rows(width):
        return [
            pl.BlockSpec((1, tm, width), lambda bi, i: (bi, i, 0)),
            pl.BlockSpec((1, HALO, width), lambda bi, i: (bi, jnp.maximum(i * hb - 1, 0), 0)),
            pl.BlockSpec((1, HALO, width),
                         lambda bi, i: (bi, jnp.minimum((i + 1) * hb, seq // HALO - 1), 0)),
        ]

    fused = prev is not None
    prev_specs = rows(W_PACK) + [mod_spec] if fused else []
    prev_args = (prev[0], prev[0], prev[0], prev[1]) if fused else ()
    return pl.pallas_call(
        functools.partial(_mix_in_kernel, tm=tm, seq=seq, fused=fused),
        grid=(b, nt),
        in_specs=rows(d) + prev_specs + [
            mod_spec,
            full(1, d), full(d, D_IN), full(GROUP, 2 * GROUP), full(N_HEADS * CHUNK, CHUNK),
            full(CHUNK, GROUP), full(3, GROUP), full(GROUP, GROUP), full(1, GROUP),
            full(1, d), full(3 * GROUP, d),
        ],
        out_specs=[
            pl.BlockSpec((1, tm, 2 * GROUP), lambda bi, i: (bi, i, 0)),
            pl.BlockSpec((1, tm, d), lambda bi, i: (bi, i, 0)),
        ],
        out_shape=[
            jax.ShapeDtypeStruct((b, seq, 2 * GROUP), BF16),
            jax.ShapeDtypeStruct((b, seq, d), BF16),
        ],
        compiler_params=pltpu.CompilerParams(
            dimension_semantics=("arbitrary", "arbitrary"), vmem_limit_bytes=VMEM_LIMIT),
        name="mix_in",
    )(x, x, x, *prev_args, mod, g1, w_in, d64, ws, bs, wc, wp, ps, gg, wo_bcd)


def _top2_of4(vals, aux):
    best, bi, ba = vals[0], jnp.zeros_like(vals[0], jnp.int32), aux[0]
    for j in range(1, 4):
        gt = vals[j] > best
        best = jnp.where(gt, vals[j], best)
        bi = jnp.where(gt, j, bi)
        ba = jnp.where(gt, aux[j], ba)
    sec = jnp.full_like(best, -jnp.inf)
    si, sa = jnp.zeros_like(bi), aux[0]
    for j in range(4):
        cand = jnp.where(bi == j, -jnp.inf, vals[j])
        gt = cand > sec
        sec = jnp.where(gt, cand, sec)
        si = jnp.where(gt, j, si)
        sa = jnp.where(gt, aux[j], sa)
    return best, sec, bi, si, ba, sa


def _mix_out_kernel(*refs, fused):
    y_ref = modp_ref = None
    if fused:
        y_ref, modp_ref = refs[5:7]
        refs = refs[:5] + refs[7:]
    (cs_ref, ss_ref, xg_ref, part_ref, x_ref, mod_ref, wf_ref, gg_ref, wo_ref,
     g2_ref, wrh_ref, wrl_ref, br_ref, tri_ref, xo_ref, h2_ref, cls_ref, rk_ref,
     cnt_ref, carry_ref) = refs

    @pl.when((pl.program_id(0) == 0) & (pl.program_id(1) == 0))
    def _():
        carry_ref[...] = jnp.zeros_like(carry_ref)

    nq = xg_ref.shape[0]
    xc = jnp.concatenate([xg_ref[q, :, 0:GROUP] for q in range(nq)], axis=1)
    xs = jnp.concatenate([xg_ref[q, :, GROUP:2 * GROUP] for q in range(nq)], axis=1)
    f_all = _dot(cs_ref[...], xc) + _dot(ss_ref[...], xs)
    for q in range(nq):
        x_in = x_ref[q]
        if fused:
            x_in = x_in + modp_ref[q][5:6] * _unpack_pairs(y_ref[q])
        _mix_out_tokens(q, f_all[:, q * GROUP:(q + 1) * GROUP], x_in, part_ref, mod_ref, wf_ref,
                        gg_ref, wo_ref, g2_ref, wrh_ref, wrl_ref, br_ref, tri_ref, xo_ref, h2_ref,
                        cls_ref, rk_ref, carry_ref)
    cnt_ref[...] = carry_ref[...]


def _mix_out_tokens(q, f, x_in, part_ref, mod_ref, wf_ref, gg_ref, wo_ref, g2_ref, wrh_ref,
                    wrl_ref, br_ref, tri_ref, xo_ref, h2_ref, cls_ref, rk_ref, carry_ref):
    ya = _dot(f.astype(BF16), wf_ref[...])
    ya = _rms(ya) * gg_ref[:, 0:GROUP]
    mix = _dot(ya.astype(BF16), wo_ref[...]) + part_ref[q].astype(F32)
    mod = mod_ref[q]
    xn = x_in + mod[2:3] * mix
    xo_ref[q] = xn
    h2 = _rms(xn) * g2_ref[...] * (1.0 + mod[4:5]) + mod[3:4]
    h_hi = h2.astype(BF16)
    h2_ref[q, :, 0:W_PACK] = _pack_pairs(h2)
    h_lo = (h2 - h_hi.astype(F32)).astype(BF16)

    both = _dot(h_hi, wrh_ref[...])
    logits = both[:, 0:128] + both[:, 128:256] + _dot(h_lo, wrl_ref[...])
    lt = logits.T[0:N_EXPERTS]
    ex = jnp.exp(lt - jnp.max(lt, axis=0, keepdims=True))
    scores = ex / jnp.sum(ex, axis=0, keepdims=True)
    biased = scores + br_ref[...]

    best_gs = None
    for g in range(N_EGROUPS):
        rows = [biased[g * 4 + j:g * 4 + j + 1] for j in range(4)]
        srow = [scores[g * 4 + j:g * 4 + j + 1] for j in range(4)]
        t1, t2, i1, i2, s1, s2 = _top2_of4(rows, srow)
        gs = t1 + t2
        if best_gs is None:
            best_gs, sel = gs, jnp.zeros_like(i1)
            bi1, bi2, bs1, bs2 = i1, i2, s1, s2
        else:
            gt = gs > best_gs
            best_gs = jnp.where(gt, gs, best_gs)
            sel = jnp.where(gt, g, sel)
            bi1, bi2 = jnp.where(gt, i1, bi1), jnp.where(gt, i2, bi2)
            bs1, bs2 = jnp.where(gt, s1, bs1), jnp.where(gt, s2, bs2)
    first_lo = bi1 < bi2
    lo = jnp.where(first_lo, bi1, bi2)
    hi = jnp.where(first_lo, bi2, bi1)
    den = bs1 + bs2
    w_a = jnp.where(first_lo, bs1, bs2) / den
    w_b = jnp.where(first_lo, bs2, bs1) / den
    pair = jnp.where(lo == 0, hi - 1, jnp.where(lo == 1, hi + 1, 5))
    cls_i = sel * len(PAIRS) + pair
    tk = cls_i.shape[1]
    gate_cols = jnp.concatenate([w_a, w_b, jnp.zeros((W_EXTRA - 2, tk), F32)], axis=0).T
    h2_ref[q, :, W_PACK:W_ROW] = lax.bitcast_convert_type(gate_cols, I32)

    onehot = lax.broadcasted_iota(jnp.int32, (CLS_PAD, tk), 0) == cls_i
    prefix = _dot(jnp.where(onehot, 1.0, 0.0).astype(BF16), tri_ref[...])
    carry = carry_ref[...]
    rank = jnp.sum(jnp.where(onehot, prefix - 1.0 + carry[:, 0:1], 0.0), axis=0, keepdims=True)
    carry_ref[...] = carry + prefix[:, tk - 1:tk]
    cls_ref[q] = cls_i
    rk_ref[q] = rank.astype(I32)


def _mix_out(cs, ss, xg, part, x, prev, mod, wf_bd, gg, wo_a, g2, wr_hi, wr_lo, b_router, tri):
    b, seq, d = x.shape
    tk = TM_MIX
    nk = seq // tk
    nq = SEQ_PER_STEP
    full = lambda *shape: pl.BlockSpec(shape, lambda k, bi: (0,) * len(shape))
    rows = lambda width: pl.BlockSpec((nq, tk, width), lambda k, bi: (bi, k, 0))
    lanes = pl.BlockSpec((nq, 1, tk), lambda k, bi: (bi, 0, k))
    mod_spec = pl.BlockSpec((nq, N_MOD, d), lambda k, bi: (bi, 0, 0))
    fused = prev is not None
    prev_specs = [rows(W_PACK), mod_spec] if fused else []
    prev_args = tuple(prev) if fused else ()
    return pl.pallas_call(
        functools.partial(_mix_out_kernel, fused=fused),
        grid=(nk, b // nq),
        in_specs=[
            pl.BlockSpec((tk, seq), lambda k, bi: (k, 0)),
            pl.BlockSpec((tk, seq), lambda k, bi: (k, 0)),
            pl.BlockSpec((nq, seq, 2 * GROUP), lambda k, bi: (bi, 0, 0)),
            rows(d), rows(d),
        ] + prev_specs + [
            mod_spec,
            full(GROUP, GROUP), full(1, d), full(GROUP, d), full(1, d),
            full(d, 256), full(d, 128), full(N_EXPERTS, 1), full(tk, tk),
        ],
        out_specs=[rows(d), rows(W_ROW), lanes, lanes, full(CLS_PAD, 128)],
        out_shape=[
            jax.ShapeDtypeStruct((b, seq, d), F32),
            jax.ShapeDtypeStruct((b, seq, W_ROW), I32),
            jax.ShapeDtypeStruct((b, 1, seq), I32),
            jax.ShapeDtypeStruct((b, 1, seq), I32),
            jax.ShapeDtypeStruct((CLS_PAD, 128), F32),
        ],
        scratch_shapes=[pltpu.VMEM((CLS_PAD, 128), F32)],
        compiler_params=pltpu.CompilerParams(
            dimension_semantics=("arbitrary", "arbitrary"), vmem_limit_bytes=VMEM_LIMIT),
        name="mix_out",
    )(cs, ss, xg, part, x, *prev_args, mod, wf_bd, gg, wo_a, g2, wr_hi, wr_lo, b_router, tri)


def _moe_kernel(ea_ref, eb_ref, nu_ref, xs_ref, wga_ref, wua_ref, wda_ref,
                wgb_ref, wub_ref, wdb_ref, o_ref):
    @pl.when(pl.program_id(0) < nu_ref[0])
    def _():
        xs = _unpack_pairs(xs_ref[:, 0:W_PACK]).astype(BF16)
        wt = lax.bitcast_convert_type(xs_ref[:, W_PACK:W_ROW], F32)

        def ffn(wg, wu, wd):
            a = _dot(xs, wg[0].astype(BF16))
            hid = (a * jax.nn.sigmoid(a)) * _dot(xs, wu[0].astype(BF16))
            return _dot(hid.astype(BF16), wd[0].astype(BF16))

        o_ref[...] = _pack_pairs(wt[:, 0:1] * ffn(wga_ref, wua_ref, wda_ref)
                                 + wt[:, 1:2] * ffn(wgb_ref, wub_ref, wdb_ref))


def _moe_ffn(tile_ea, tile_eb, n_used, xs, wg, wu, wd):
    p = xs.shape[0]
    d = D_MODEL
    tm = TM_MOE
    nt = p // tm
    row = lambda i, ea, eb, nu: (jnp.minimum(i, nu[0] - 1), 0)
    exp_a = lambda i, ea, eb, nu: (ea[i], 0, 0)
    exp_b = lambda i, ea, eb, nu: (eb[i], 0, 0)
    return pl.pallas_call(
        _moe_kernel,
        grid_spec=pltpu.PrefetchScalarGridSpec(
            num_scalar_prefetch=3,
            grid=(nt,),
            in_specs=[
                pl.BlockSpec((tm, W_ROW), row),
                pl.BlockSpec((1, d, D_EXPERT), exp_a),
                pl.BlockSpec((1, d, D_EXPERT), exp_a),
                pl.BlockSpec((1, D_EXPERT, d), exp_a),
                pl.BlockSpec((1, d, D_EXPERT), exp_b),
                pl.BlockSpec((1, d, D_EXPERT), exp_b),
                pl.BlockSpec((1, D_EXPERT, d), exp_b),
            ],
            out_specs=pl.BlockSpec((tm, W_PACK), row),
        ),
        out_shape=jax.ShapeDtypeStruct((p, W_PACK), I32),
        compiler_params=pltpu.CompilerParams(
            dimension_semantics=("arbitrary",), vmem_limit_bytes=VMEM_LIMIT),
        name="moe_ffn",
    )(tile_ea, tile_eb, n_used, xs, wg, wu, wd, wg, wu, wd)


def _sc_worker_id():
    return lax.axis_index("s") * SC_CORES + lax.axis_index("c")


def _sc_gather_rows(table_hbm, out_hbm, idx_all, out_base, n_chunks, idx_bufs, row_bufs, sems):
    def copy(slot):
        return pltpu.make_async_copy(table_hbm.at[idx_bufs[slot]], row_bufs[slot], sems[slot])

    def start(j, slot):
        for q in range(SC_ROWS // SC_LANES):
            idx_bufs[slot][pl.ds(q * SC_LANES, SC_LANES)] = (
                idx_all[pl.ds(j * SC_ROWS + q * SC_LANES, SC_LANES)])
        copy(slot).start()

    def flush(j, slot):
        copy(slot).wait()
        pltpu.sync_copy(row_bufs[slot], out_hbm.at[pl.ds(out_base + j * SC_ROWS, SC_ROWS)])

    start(0, 0)

    @pl.loop(0, n_chunks // 2)
    def _(jj):
        j = 2 * jj
        start(j + 1, 1)
        flush(j, 0)

        @pl.when(j + 2 < n_chunks)
        def _():
            start(j + 2, 0)

        flush(j + 1, 1)


def _sc_scratch(n_idx, width):
    return [
        pltpu.VMEM((n_idx,), I32),
        pltpu.VMEM((SC_ROWS,), I32), pltpu.VMEM((SC_ROWS,), I32),
        pltpu.VMEM((SC_ROWS, width), I32), pltpu.VMEM((SC_ROWS, width), I32),
        pltpu.SemaphoreType.DMA, pltpu.SemaphoreType.DMA,
    ]


def _sc_dispatch(cls, rank, pstart, table, p):
    t, width = table.shape
    tw = t // SC_WORKERS
    n_chunks = tw // SC_ROWS
    assert t % (SC_WORKERS * 2 * SC_ROWS) == 0
    mesh = plsc.VectorSubcoreMesh(core_axis_name="c", subcore_axis_name="s")

    @functools.partial(
        pl.kernel, mesh=mesh,
        out_type=[jax.ShapeDtypeStruct((p, width), I32), jax.ShapeDtypeStruct((t,), I32)],
        scratch_types=[pltpu.VMEM((tw,), I32), pltpu.VMEM((CLS_PAD,), I32)]
        + _sc_scratch(tw, width) + [pltpu.SemaphoreType.DMA, pltpu.SemaphoreType.DMA],
        compiler_params=pltpu.CompilerParams(needs_layout_passes=False),
        name="dispatch")
    def k(cls_hbm, rank_hbm, ps_hbm, table_hbm, xs_hbm, inv_hbm,
          cls_v, ps_v, inv_v, idx_a, idx_b, rows_a, rows_b, rsem_a, rsem_b, wsem_a, wsem_b):
        base = _sc_worker_id() * tw
        pltpu.sync_copy(ps_hbm, ps_v)
        pltpu.sync_copy(cls_hbm.at[pl.ds(base, tw)], cls_v)
        pltpu.sync_copy(rank_hbm.at[pl.ds(base, tw)], inv_v)

        @pl.loop(0, tw // SC_LANES)
        def _(i):
            sl = pl.ds(i * SC_LANES, SC_LANES)
            inv_v[sl] = plsc.load_gather(ps_v, [cls_v[sl]]) + inv_v[sl]

        pltpu.sync_copy(inv_v, inv_hbm.at[pl.ds(base, tw)])

        slots = ((idx_a, rows_a, rsem_a, wsem_a), (idx_b, rows_b, rsem_b, wsem_b))

        def read(j, slot):
            _, rows, rsem, _ = slots[slot]
            return pltpu.make_async_copy(
                table_hbm.at[pl.ds(base + j * SC_ROWS, SC_ROWS)], rows, rsem)

        def write(slot):
            idx, rows, _, wsem = slots[slot]
            return pltpu.make_async_copy(rows, xs_hbm.at[idx], wsem)

        @pl.loop(0, n_chunks // 2)
        def _(jj):
            for slot in range(2):
                j = 2 * jj + slot
                read(j, slot).start()
                for q in range(SC_ROWS // SC_LANES):
                    slots[slot][0][pl.ds(q * SC_LANES, SC_LANES)] = (
                        inv_v[pl.ds(j * SC_ROWS + q * SC_LANES, SC_LANES)])
            for slot in range(2):
                read(2 * jj + slot, slot).wait()
                write(slot).start()
            for slot in range(2):
                write(slot).wait()

    return k(cls, rank, pstart, table)


def _sc_unsort(table, inv):
    t = inv.shape[0]
    width = table.shape[1]
    tw = t // SC_WORKERS
    n_chunks = tw // SC_ROWS
    assert t % (SC_WORKERS * 2 * SC_ROWS) == 0
    mesh = plsc.VectorSubcoreMesh(core_axis_name="c", subcore_axis_name="s")

    @functools.partial(
        pl.kernel, mesh=mesh, out_type=jax.ShapeDtypeStruct((t, width), I32),
        scratch_types=_sc_scratch(tw, width),
        compiler_params=pltpu.CompilerParams(needs_layout_passes=False),
        name="unsort")
    def k(table_hbm, inv_hbm, out_hbm, idx_all, idx_a, idx_b, rows_a, rows_b, sem_a, sem_b):
        base = _sc_worker_id() * tw
        pltpu.sync_copy(inv_hbm.at[pl.ds(base, tw)], idx_all)
        _sc_gather_rows(table_hbm, out_hbm, idx_all, base, n_chunks,
                        (idx_a, idx_b), (rows_a, rows_b), (sem_a, sem_b))

    return k(table, inv)


def _combine_kernel(x_ref, y_ref, mod_ref, gf_ref, o_ref, *, final):
    xn = x_ref[0] + mod_ref[0][5:6] * _unpack_pairs(y_ref[0])
    if final:
        xn = _rms(xn) * gf_ref[...]
    o_ref[0] = xn


def _combine(x, y, mod, g_final, final):
    b, seq, d = x.shape
    tm = TM_COMBINE
    blk = pl.BlockSpec((1, tm, d), lambda bi, i: (bi, i, 0))
    return pl.pallas_call(
        functools.partial(_combine_kernel, final=final),
        grid=(b, seq // tm),
        in_specs=[blk, pl.BlockSpec((1, tm, W_PACK), lambda bi, i: (bi, i, 0)),
                  pl.BlockSpec((1, N_MOD, d), lambda bi, i: (bi, 0, 0)),
                  pl.BlockSpec((1, d), lambda bi, i: (0, 0))],
        out_specs=blk,
        out_shape=jax.ShapeDtypeStruct((b, seq, d), F32),
        compiler_params=pltpu.CompilerParams(
            dimension_semantics=("arbitrary", "arbitrary"), vmem_limit_bytes=VMEM_LIMIT),
        name="combine",
    )(x, y, mod, g_final)


def _dft_tables(seq):
    n = np.arange(seq)
    ang = 2.0 * np.pi * ((n[:, None] * n[None, :]) % seq) / seq
    scale = 1.0 / np.sqrt(seq)
    return (np.cos(ang) * scale).astype(np.float32), (-np.sin(ang) * scale).astype(np.float32)


def _dft64_blockdiag():
    n = np.arange(HEAD_DIM)
    ang = 2.0 * np.pi * ((n[:, None] * n[None, :]) % HEAD_DIM) / HEAD_DIM
    c, s = np.cos(ang) / np.sqrt(HEAD_DIM), np.sin(ang) / np.sqrt(HEAD_DIM)
    bd = np.zeros((GROUP, 2 * GROUP), np.float32)
    for h in range(N_HEADS):
        r = slice(h * HEAD_DIM, (h + 1) * HEAD_DIM)
        bd[r, h * HEAD_DIM:(h + 1) * HEAD_DIM] = c
        bd[r, GROUP + h * HEAD_DIM:GROUP + (h + 1) * HEAD_DIM] = s
    return bd


def _block_diag(w):
    n, k, _ = w.shape
    eye = jnp.eye(n, dtype=w.dtype)
    return (eye[:, None, :, None] * w[:, :, None, :]).reshape(n * k, n * k)


def _route_tables(counts, t, tm):
    nt = t // tm + N_CLASSES
    nt += -nt % 8
    ntile_c = (counts + tm - 1) // tm
    tile_end = jnp.cumsum(ntile_c)
    pstart = (tile_end - ntile_c) * tm
    n_used = tile_end[-1]
    tile_ids = jnp.minimum(jnp.arange(nt, dtype=jnp.int32), n_used - 1)
    tile_cls = jnp.sum((tile_ids[:, None] >= tile_end[None, :]).astype(jnp.int32), axis=1)
    pair_lo = jnp.array([p[0] for p in PAIRS], jnp.int32)
    pair_hi = jnp.array([p[1] for p in PAIRS], jnp.int32)
    grp, pid = tile_cls // len(PAIRS), tile_cls % len(PAIRS)
    tile_ea = grp * EXPERTS_PER_GROUP + jnp.take(pair_lo, pid)
    tile_eb = grp * EXPERTS_PER_GROUP + jnp.take(pair_hi, pid)
    pstart = jnp.concatenate([pstart, jnp.zeros((CLS_PAD - N_CLASSES,), I32)])
    return pstart, tile_ea, tile_eb, n_used.reshape(1).astype(jnp.int32), nt


def _trunk(x, mods, lw, shared):
    b, seq, d = x.shape
    t = b * seq
    depth = len(lw)
    prev = None
    for l in range(depth):
        w = lw[l]
        mod = mods[l]
        xg, part = _mix_in(x, prev, mod, w["g1"], w["w_in"], shared["d64"], w["ws"], w["bs"],
                           w["wc"], w["wp"], w["ps"], w["gg"], w["wo_bcd"])
        x_mid, h2, cls, rank, cnt = _mix_out(
            shared["cs"], shared["ss"], xg, part, x, prev, mod, w["wf"], w["gg"], w["wo_a"],
            w["g2"], shared["wr_both"], shared["wr_hi"], shared["b_router"], shared["tri"])
        counts = cnt[:N_CLASSES, 0].astype(I32)
        pstart, tile_ea, tile_eb, n_used, nt = _route_tables(counts, t, TM_MOE)
        xs, inv = _sc_dispatch(cls.reshape(t), rank.reshape(t), pstart, h2.reshape(t, W_ROW),
                               nt * TM_MOE)
        ys = _moe_ffn(tile_ea + l * N_EXPERTS, tile_eb + l * N_EXPERTS, n_used, xs,
                      shared["wg"], shared["wu"], shared["wd"])
        y_tok = _sc_unsort(ys, inv).reshape(b, seq, W_PACK)
        x, prev = x_mid, (y_tok, mod)
    return _combine(x, prev[0], prev[1], shared["g_final"], final=True)


def kernel(x_prompt, x_sample, c_prompt, c_sample, w_ada, b_ada, g_norm1, w_in, w_fourier, w_spatial, b_spatial, w_conv, w_pool, pool_scale, g_group, w_out, g_norm2, w_router, b_router, w_exp_gate, w_exp_up, w_exp_down, g_final):
    depth = w_in.shape[0]
    seq = x_prompt.shape[1]
    d = D_MODEL
    nb_p = c_prompt.shape[0]

    cs, ss = _dft_tables(seq)
    w_router = jnp.pad(w_router, ((0, 0), (0, 128 - N_EXPERTS)))
    wr_hi = w_router.astype(BF16)
    wr_lo = (w_router - wr_hi.astype(F32)).astype(BF16)
    shared = {
        "cs": jnp.asarray(cs).astype(BF16),
        "ss": jnp.asarray(ss).astype(BF16),
        "d64": jnp.asarray(_dft64_blockdiag()).astype(BF16),
        "wr_both": jnp.concatenate([wr_hi, wr_lo], axis=1), "wr_hi": wr_hi,
        "b_router": b_router.reshape(N_EXPERTS, 1).astype(F32),
        "g_final": g_final.reshape(1, d),
        "tri": jnp.asarray(np.triu(np.ones((TM_MIX, TM_MIX), np.float32))).astype(BF16),
    }
    lw = []
    for l in range(depth):
        lw.append({
            "g1": g_norm1[l].reshape(1, d),
            "w_in": w_in[l].astype(BF16),
            "ws": w_spatial[l].reshape(N_HEADS * CHUNK, CHUNK).astype(BF16),
            "bs": jnp.repeat(b_spatial[l].T, HEAD_DIM, axis=1),
            "wc": w_conv[l],
            "wp": _block_diag(w_pool[l]).astype(BF16),
            "ps": pool_scale[l].reshape(1, GROUP),
            "gg": g_group[l].reshape(1, d),
            "wo_bcd": w_out[l, GROUP:].astype(BF16),
            "wo_a": w_out[l, :GROUP].astype(BF16),
            "wf": _block_diag(w_fourier[l]).astype(BF16),
            "g2": g_norm2[l].reshape(1, d),
        })
    shared["wg"] = w_exp_gate.reshape(depth * N_EXPERTS, d, D_EXPERT)
    shared["wu"] = w_exp_up.reshape(depth * N_EXPERTS, d, D_EXPERT)
    shared["wd"] = w_exp_down.reshape(depth * N_EXPERTS, D_EXPERT, d)

    c_all = jnp.concatenate([c_prompt, c_sample], axis=0)
    mod_all = _ada(c_all, w_ada, b_ada).reshape(depth, c_all.shape[0], N_MOD, d)
    mods_p = [mod_all[l, :nb_p] for l in range(depth)]
    mods_s = [mod_all[l, nb_p:] for l in range(depth)]
    y_prompt = _trunk(x_prompt, mods_p, lw, shared)
    y_sample = _trunk(x_sample, mods_s, lw, shared)
    return (y_prompt, y_sample)
```

```python
import functools

import numpy as np
import jax
import jax.numpy as jnp
from jax import lax
from jax.experimental import pallas as pl
from jax.experimental.pallas import tpu as pltpu
from jax.experimental.pallas import tpu_sc as plsc

F32 = jnp.float32
BF16 = jnp.bfloat16
I32 = jnp.int32
U32 = jnp.uint32

D_MODEL = 1024
GROUP = 256
N_HEADS = 4
HEAD_DIM = 64
CHUNK = 128
D_IN = 7 * GROUP
N_EXPERTS = 16
N_EGROUPS = 4
EXPERTS_PER_GROUP = 4
D_EXPERT = 512
N_MOD = 6
EPS = 1e-6
HALO = 8
PAIRS = ((0, 1), (0, 2), (0, 3), (1, 3), (1, 2), (2, 3))
N_CLASSES = N_EGROUPS * len(PAIRS)
CLS_PAD = 32

TM_IN = 512
TM_MIX = 512
TM_MOE = 512
SEQ_PER_STEP = 2
TM_COMBINE = 1024
VMEM_LIMIT = 56 * 1024 * 1024

W_PACK = D_MODEL // 2
W_EXTRA = 128
W_ROW = W_PACK + W_EXTRA

SC_CORES = 2
SC_SUBCORES = 16
SC_LANES = 16
SC_WORKERS = SC_CORES * SC_SUBCORES
SC_ROWS = 64
SC_SCAN = 2048


def _rms(x):
    return x * lax.rsqrt(jnp.mean(x * x, axis=-1, keepdims=True) + EPS)


def _dot(a, b):
    return jnp.dot(a, b, preferred_element_type=F32)


def _pack_pairs(x):
    k = x.shape[1] // 2
    xb = x.astype(BF16).astype(F32)
    hi = lax.bitcast_convert_type(xb[:, :k], U32)
    lo = lax.bitcast_convert_type(xb[:, k:], U32)
    return lax.bitcast_convert_type(hi | (lo >> 16), I32)


def _unpack_pairs(w):
    u = lax.bitcast_convert_type(w, U32)
    hi = lax.bitcast_convert_type(u & jnp.uint32(0xFFFF0000), F32)
    lo = lax.bitcast_convert_type(u << 16, F32)
    return jnp.concatenate([hi, lo], axis=1)


def _ada_kernel(c_ref, w_ref, b_ref, o_ref):
    c = c_ref[...]
    sc = c * jax.nn.sigmoid(c)
    o_ref[0] = _dot(sc.astype(BF16), w_ref[0].astype(BF16)) + b_ref[0]


def _ada(c, w_ada, b_ada):
    depth, d, n = w_ada.shape
    bc = c.shape[0]
    tn = 1536
    return pl.pallas_call(
        _ada_kernel,
        grid=(depth, n // tn),
        in_specs=[
            pl.BlockSpec((bc, d), lambda l, j: (0, 0)),
            pl.BlockSpec((1, d, tn), lambda l, j: (l, 0, j)),
            pl.BlockSpec((1, 1, tn), lambda l, j: (l, 0, j)),
        ],
        out_specs=pl.BlockSpec((1, bc, tn), lambda l, j: (l, 0, j)),
        out_shape=jax.ShapeDtypeStruct((depth, bc, n), F32),
        compiler_params=pltpu.CompilerParams(
            dimension_semantics=("arbitrary", "arbitrary"), vmem_limit_bytes=VMEM_LIMIT),
        name="ada_mod",
    )(c, w_ada, b_ada.reshape(depth, 1, n))


def _mix_in_kernel(*refs, tm, seq, fused):
    if fused:
        ym_ref, yp_ref, yn_ref, modp_ref = refs[3:7]
        refs = refs[:3] + refs[7:]
    (xm_ref, xp_ref, xn_ref, mod_ref, g1_ref, win_ref, d64_ref, ws_ref, bs_ref,
     wc_ref, wp_ref, ps_ref, gg_ref, wo_ref, xg_ref, part_ref) = refs
    i = pl.program_id(1)
    ne = tm + 2 * HALO
    xe = jnp.concatenate([xp_ref[0], xm_ref[0], xn_ref[0]], axis=0)
    if fused:
        ye = jnp.concatenate([yp_ref[0], ym_ref[0], yn_ref[0]], axis=0)
        xe = xe + modp_ref[0][5:6] * _unpack_pairs(ye)
    mod = mod_ref[0]
    h = _rms(xe) * g1_ref[...] * (1.0 + mod[1:2]) + mod[0:1]
    proj = _dot(h.astype(BF16), win_ref[...])

    gpos = lax.broadcasted_iota(jnp.int32, (ne, GROUP), 0) + (i * tm - HALO)
    valid = (gpos >= 0) & (gpos < seq)
    pm = proj[HALO:HALO + tm]

    xg_ref[0] = _dot(pm[:, 0:GROUP].astype(BF16), d64_ref[...]).astype(BF16)

    u = pm[:, GROUP:2 * GROUP]
    v = pm[:, 2 * GROUP:3 * GROUP]
    head = lax.broadcasted_iota(jnp.int32, (CHUNK, GROUP), 1) // HEAD_DIM
    yb_chunks = []
    for c in range(tm // CHUNK):
        vc = v[c * CHUNK:(c + 1) * CHUNK].astype(BF16)
        m_all = _dot(ws_ref[...], vc)
        mixed = bs_ref[...]
        for hh in range(N_HEADS):
            mixed = mixed + jnp.where(head == hh, m_all[hh * CHUNK:(hh + 1) * CHUNK], 0.0)
        yb_chunks.append(u[c * CHUNK:(c + 1) * CHUNK] * mixed)
    yb = jnp.concatenate(yb_chunks, axis=0)

    z = jnp.where(valid, proj[:, 4 * GROUP:5 * GROUP] * proj[:, 5 * GROUP:6 * GROUP], 0.0)
    conv = (pltpu.roll(z, 1, 0)[HALO:HALO + tm] * wc_ref[0:1, :]
            + z[HALO:HALO + tm] * wc_ref[1:2, :]
            + pltpu.roll(z, ne - 1, 0)[HALO:HALO + tm] * wc_ref[2:3, :])
    yc = pm[:, 3 * GROUP:4 * GROUP] * conv

    p = jnp.where(valid, proj[:, 6 * GROUP:7 * GROUP], 0.0)
    a2 = p + pltpu.roll(p, 1, 0)
    a4 = a2 + pltpu.roll(a2, 2, 0)
    a8 = a4 + pltpu.roll(a4, 4, 0)
    a16 = a8 + pltpu.roll(a8, 8, 0)
    w2 = a2[HALO:HALO + tm]
    w4 = pltpu.roll(a4, ne - 1, 0)[HALO:HALO + tm]
    w8 = pltpu.roll(a8, ne - 3, 0)[HALO:HALO + tm]
    w16 = pltpu.roll(a16, ne - 7, 0)[HALO:HALO + tm]
    grp = lax.broadcasted_iota(jnp.int32, (tm, GROUP), 1) // HEAD_DIM
    wsum = jnp.where(grp == 0, w2, jnp.where(grp == 1, w4, jnp.where(grp == 2, w8, w16)))
    left = jnp.left_shift(1, grp)
    t = lax.broadcasted_iota(jnp.int32, (tm, GROUP), 0) + i * tm
    cnt = jnp.minimum(t + left, seq) - jnp.maximum(t - left, 0)
    pooled = wsum / cnt.astype(F32) - p[HALO:HALO + tm]
    yd = _dot(pooled.astype(BF16), wp_ref[...]) * ps_ref[...]

    gg = gg_ref[...]
    ycat = jnp.concatenate([
        (_rms(yb) * gg[:, GROUP:2 * GROUP]).astype(BF16),
        (_rms(yc) * gg[:, 2 * GROUP:3 * GROUP]).astype(BF16),
        (_rms(yd) * gg[:, 3 * GROUP:4 * GROUP]).astype(BF16)], axis=1)
    part_ref[0] = _dot(ycat, wo_ref[...]).astype(part_ref.dtype)


def _mix_in(x, prev, mod, g1, w_in, d64, ws, bs, wc, wp, ps, gg, wo_bcd):
    b, seq, d = x.shape
    tm = TM_IN
    nt = seq // tm
    hb = tm // HALO
    full = lambda *shape: pl.BlockSpec(shape, lambda bi, i: (0,) * len(shape))
    mod_spec = pl.BlockSpec((1, N_MOD, d), lambda bi, i: (bi, 0, 0))

    def rows(width):
        return [
            pl.BlockSpec((1, tm, width), lambda bi, i: (bi, i, 0)),
            pl.BlockSpec((1, HALO, width), lambda bi, i: (bi, jnp.maximum(i * hb - 1, 0), 0)),
            pl.BlockSpec((1, HALO, width),
                         lambda bi, i: (bi, jnp.minimum((i + 1) * hb, seq // HALO - 1), 0)),
        ]

    fused = prev is not None
    prev_specs = rows(W_PACK) + [mod_spec] if fused else []
    prev_args = (prev[0], prev[0], prev[0], prev[1]) if fused else ()
    return pl.pallas_call(
        functools.partial(_mix_in_kernel, tm=tm, seq=seq, fused=fused),
        grid=(b, nt),
        in_specs=rows(d) + prev_specs + [
            mod_spec,
            full(1, d), full(d, D_IN), full(GROUP, 2 * GROUP), full(N_HEADS * CHUNK, CHUNK),
            full(CHUNK, GROUP), full(3, GROUP), full(GROUP, GROUP), full(1, GROUP),
            full(1, d), full(3 * GROUP, d),
        ],
        out_specs=[
            pl.BlockSpec((1, tm, 2 * GROUP), lambda bi, i: (bi, i, 0)),
            pl.BlockSpec((1, tm, d), lambda bi, i: (bi, i, 0)),
        ],
        out_shape=[
            jax.ShapeDtypeStruct((b, seq, 2 * GROUP), BF16),
            jax.ShapeDtypeStruct((b, seq, d), BF16),
        ],
        compiler_params=pltpu.CompilerParams(
            dimension_semantics=("arbitrary", "arbitrary"), vmem_limit_bytes=VMEM_LIMIT),
        name="mix_in",
    )(x, x, x, *prev_args, mod, g1, w_in, d64, ws, bs, wc, wp, ps, gg, wo_bcd)


def _top2_of4(vals, aux):
    best, bi, ba = vals[0], jnp.zeros_like(vals[0], jnp.int32), aux[0]
    for j in range(1, 4):
        gt = vals[j] > best
        best = jnp.where(gt, vals[j], best)
        bi = jnp.where(gt, j, bi)
        ba = jnp.where(gt, aux[j], ba)
    sec = jnp.full_like(best, -jnp.inf)
    si, sa = jnp.zeros_like(bi), aux[0]
    for j in range(4):
        cand = jnp.where(bi == j, -jnp.inf, vals[j])
        gt = cand > sec
        sec = jnp.where(gt, cand, sec)
        si = jnp.where(gt, j, si)
        sa = jnp.where(gt, aux[j], sa)
    return best, sec, bi, si, ba, sa


def _mix_out_kernel(*refs, fused):
    y_ref = modp_ref = None
    if fused:
        y_ref, modp_ref = refs[5:7]
        refs = refs[:5] + refs[7:]
    (cs_ref, ss_ref, xg_ref, part_ref, x_ref, mod_ref, wf_ref, gg_ref, wo_ref,
     g2_ref, wrh_ref, wrl_ref, br_ref, tri_ref, xo_ref, h2_ref, cls_ref, rk_ref,
     cnt_ref, carry_ref) = refs

    @pl.when((pl.program_id(0) == 0) & (pl.program_id(1) == 0))
    def _():
        carry_ref[...] = jnp.zeros_like(carry_ref)

    nq = xg_ref.shape[0]
    xc = jnp.concatenate([xg_ref[q, :, 0:GROUP] for q in range(nq)], axis=1)
    xs = jnp.concatenate([xg_ref[q, :, GROUP:2 * GROUP] for q in range(nq)], axis=1)
    f_all = _dot(cs_ref[...], xc) + _dot(ss_ref[...], xs)
    for q in range(nq):
        x_in = x_ref[q]
        if fused:
            x_in = x_in + modp_ref[q][5:6] * _unpack_pairs(y_ref[q])
        _mix_out_tokens(q, f_all[:, q * GROUP:(q + 1) * GROUP], x_in, part_ref, mod_ref, wf_ref,
                        gg_ref, wo_ref, g2_ref, wrh_ref, wrl_ref, br_ref, tri_ref, xo_ref, h2_ref,
                        cls_ref, rk_ref, carry_ref)
    cnt_ref[...] = carry_ref[...]


def _mix_out_tokens(q, f, x_in, part_ref, mod_ref, wf_ref, gg_ref, wo_ref, g2_ref, wrh_ref,
                    wrl_ref, br_ref, tri_ref, xo_ref, h2_ref, cls_ref, rk_ref, carry_ref):
    ya = _dot(f.astype(BF16), wf_ref[...])
    ya = _rms(ya) * gg_ref[:, 0:GROUP]
    mix = _dot(ya.astype(BF16), wo_ref[...]) + part_ref[q].astype(F32)
    mod = mod_ref[q]
    xn = x_in + mod[2:3] * mix
    xo_ref[q] = xn
    h2 = _rms(xn) * g2_ref[...] * (1.0 + mod[4:5]) + mod[3:4]
    h_hi = h2.astype(BF16)
    h2_ref[q, :, 0:W_PACK] = _pack_pairs(h2)
    h_lo = (h2 - h_hi.astype(F32)).astype(BF16)

    both = _dot(h_hi, wrh_ref[...])
    logits = both[:, 0:128] + both[:, 128:256] + _dot(h_lo, wrl_ref[...])
    lt = logits.T[0:N_EXPERTS]
    ex = jnp.exp(lt - jnp.max(lt, axis=0, keepdims=True))
    scores = ex / jnp.sum(ex, axis=0, keepdims=True)
    biased = scores + br_ref[...]

    best_gs = None
    for g in range(N_EGROUPS):
        rows = [biased[g * 4 + j:g * 4 + j + 1] for j in range(4)]
        srow = [scores[g * 4 + j:g * 4 + j + 1] for j in range(4)]
        t1, t2, i1, i2, s1, s2 = _top2_of4(rows, srow)
        gs = t1 + t2
        if best_gs is None:
            best_gs, sel = gs, jnp.zeros_like(i1)
            bi1, bi2, bs1, bs2 = i1, i2, s1, s2
        else:
            gt = gs > best_gs
            best_gs = jnp.where(gt, gs, best_gs)
            sel = jnp.where(gt, g, sel)
            bi1, bi2 = jnp.where(gt, i1, bi1), jnp.where(gt, i2, bi2)
            bs1, bs2 = jnp.where(gt, s1, bs1), jnp.where(gt, s2, bs2)
    first_lo = bi1 < bi2
    lo = jnp.where(first_lo, bi1, bi2)
    hi = jnp.where(first_lo, bi2, bi1)
    den = bs1 + bs2
    w_a = jnp.where(first_lo, bs1, bs2) / den
    w_b = jnp.where(first_lo, bs2, bs1) / den
    pair = jnp.where(lo == 0, hi - 1, jnp.where(lo == 1, 6 - hi, 5))
    cls_i = sel * len(PAIRS) + pair
    tk = cls_i.shape[1]
    gate_cols = jnp.concatenate([w_a, w_b, jnp.zeros((W_EXTRA - 2, tk), F32)], axis=0).T
    h2_ref[q, :, W_PACK:W_ROW] = lax.bitcast_convert_type(gate_cols, I32)

    onehot = lax.broadcasted_iota(jnp.int32, (CLS_PAD, tk), 0) == cls_i
    prefix = _dot(jnp.where(onehot, 1.0, 0.0).astype(BF16), tri_ref[...])
    carry = carry_ref[...]
    rank = jnp.sum(jnp.where(onehot, prefix - 1.0 + carry[:, 0:1], 0.0), axis=0, keepdims=True)
    carry_ref[...] = carry + prefix[:, tk - 1:tk]
    cls_ref[q] = cls_i
    rk_ref[q] = rank.astype(I32)


def _mix_out(cs, ss, xg, part, x, prev, mod, wf_bd, gg, wo_a, g2, wr_hi, wr_lo, b_router, tri):
    b, seq, d = x.shape
    tk = TM_MIX
    nk = seq // tk
    nq = SEQ_PER_STEP
    full = lambda *shape: pl.BlockSpec(shape, lambda k, bi: (0,) * len(shape))
    rows = lambda width: pl.BlockSpec((nq, tk, width), lambda k, bi: (bi, k, 0))
    lanes = pl.BlockSpec((nq, 1, tk), lambda k, bi: (bi, 0, k))
    mod_spec = pl.BlockSpec((nq, N_MOD, d), lambda k, bi: (bi, 0, 0))
    fused = prev is not None
    prev_specs = [rows(W_PACK), mod_spec] if fused else []
    prev_args = tuple(prev) if fused else ()
    return pl.pallas_call(
        functools.partial(_mix_out_kernel, fused=fused),
        grid=(nk, b // nq),
        in_specs=[
            pl.BlockSpec((tk, seq), lambda k, bi: (k, 0)),
            pl.BlockSpec((tk, seq), lambda k, bi: (k, 0)),
            pl.BlockSpec((nq, seq, 2 * GROUP), lambda k, bi: (bi, 0, 0)),
            rows(d), rows(d),
        ] + prev_specs + [
            mod_spec,
            full(GROUP, GROUP), full(1, d), full(GROUP, d), full(1, d),
            full(d, 256), full(d, 128), full(N_EXPERTS, 1), full(tk, tk),
        ],
        out_specs=[rows(d), rows(W_ROW), lanes, lanes, full(CLS_PAD, 128)],
        out_shape=[
            jax.ShapeDtypeStruct((b, seq, d), F32),
            jax.ShapeDtypeStruct((b, seq, W_ROW), I32),
            jax.ShapeDtypeStruct((b, 1, seq), I32),
            jax.ShapeDtypeStruct((b, 1, seq), I32),
            jax.ShapeDtypeStruct((CLS_PAD, 128), F32),
        ],
        scratch_shapes=[pltpu.VMEM((CLS_PAD, 128), F32)],
        compiler_params=pltpu.CompilerParams(
            dimension_semantics=("arbitrary", "arbitrary"), vmem_limit_bytes=VMEM_LIMIT),
        name="mix_out",
    )(cs, ss, xg, part, x, *prev_args, mod, wf_bd, gg, wo_a, g2, wr_hi, wr_lo, b_router, tri)


def _moe_kernel(ea_ref, eb_ref, nu_ref, xs_ref, wga_ref, wua_ref, wda_ref,
                wgb_ref, wub_ref, wdb_ref, o_ref):
    @pl.when(pl.program_id(0) < nu_ref[0])
    def _():
        xs = _unpack_pairs(xs_ref[:, 0:W_PACK]).astype(BF16)
        wt = lax.bitcast_convert_type(xs_ref[:, W_PACK:W_ROW], F32)

        def ffn(wg, wu, wd):
            a = _dot(xs, wg[0].astype(BF16))
            hid = (a * jax.nn.sigmoid(a)) * _dot(xs, wu[0].astype(BF16))
            return _dot(hid.astype(BF16), wd[0].astype(BF16))

        o_ref[...] = _pack_pairs(wt[:, 0:1] * ffn(wga_ref, wua_ref, wda_ref)
                                 + wt[:, 1:2] * ffn(wgb_ref, wub_ref, wdb_ref))


def _moe_ffn(tile_ea, tile_eb, n_used, xs, wg, wu, wd):
    p = xs.shape[0]
    d = D_MODEL
    tm = TM_MOE
    nt = p // tm
    row = lambda i, ea, eb, nu: (jnp.minimum(i, nu[0] - 1), 0)
    exp_a = lambda i, ea, eb, nu: (ea[i], 0, 0)
    exp_b = lambda i, ea, eb, nu: (eb[i], 0, 0)
    return pl.pallas_call(
        _moe_kernel,
        grid_spec=pltpu.PrefetchScalarGridSpec(
            num_scalar_prefetch=3,
            grid=(nt,),
            in_specs=[
                pl.BlockSpec((tm, W_ROW), row),
                pl.BlockSpec((1, d, D_EXPERT), exp_a),
                pl.BlockSpec((1, d, D_EXPERT), exp_a),
                pl.BlockSpec((1, D_EXPERT, d), exp_a),
                pl.BlockSpec((1, d, D_EXPERT), exp_b),
                pl.BlockSpec((1, d, D_EXPERT), exp_b),
                pl.BlockSpec((1, D_EXPERT, d), exp_b),
            ],
            out_specs=pl.BlockSpec((tm, W_PACK), row),
        ),
        out_shape=jax.ShapeDtypeStruct((p, W_PACK), I32),
        compiler_params=pltpu.CompilerParams(
            dimension_semantics=("arbitrary",), vmem_limit_bytes=VMEM_LIMIT),
        name="moe_ffn",
    )(tile_ea, tile_eb, n_used, xs, wg, wu, wd, wg, wu, wd)


def _sc_worker_id():
    return lax.axis_index("s") * SC_CORES + lax.axis_index("c")


def _sc_gather_rows(table_hbm, out_hbm, idx_all, out_base, n_chunks, idx_bufs, row_bufs, sems):
    def copy(slot):
        return pltpu.make_async_copy(table_hbm.at[idx_bufs[slot]], row_bufs[slot], sems[slot])

    def start(j, slot):
        for q in range(SC_ROWS // SC_LANES):
            idx_bufs[slot][pl.ds(q * SC_LANES, SC_LANES)] = (
                idx_all[pl.ds(j * SC_ROWS + q * SC_LANES, SC_LANES)])
        copy(slot).start()

    def flush(j, slot):
        copy(slot).wait()
        pltpu.sync_copy(row_bufs[slot], out_hbm.at[pl.ds(out_base + j * SC_ROWS, SC_ROWS)])

    start(0, 0)

    @pl.loop(0, n_chunks // 2)
    def _(jj):
        j = 2 * jj
        start(j + 1, 1)
        flush(j, 0)

        @pl.when(j + 2 < n_chunks)
        def _():
            start(j + 2, 0)

        flush(j + 1, 1)


def _sc_scratch(n_idx, width):
    return [
        pltpu.VMEM((n_idx,), I32),
        pltpu.VMEM((SC_ROWS,), I32), pltpu.VMEM((SC_ROWS,), I32),
        pltpu.VMEM((SC_ROWS, width), I32), pltpu.VMEM((SC_ROWS, width), I32),
        pltpu.SemaphoreType.DMA, pltpu.SemaphoreType.DMA,
    ]


def _sc_dispatch(cls, rank, pstart, table, p):
    t, width = table.shape
    tw = t // SC_WORKERS
    n_chunks = tw // SC_ROWS
    assert t % (SC_WORKERS * 2 * SC_ROWS) == 0
    mesh = plsc.VectorSubcoreMesh(core_axis_name="c", subcore_axis_name="s")

    @functools.partial(
        pl.kernel, mesh=mesh,
        out_type=[jax.ShapeDtypeStruct((p, width), I32), jax.ShapeDtypeStruct((t,), I32)],
        scratch_types=[pltpu.VMEM((tw,), I32), pltpu.VMEM((CLS_PAD,), I32)]
        + _sc_scratch(tw, width) + [pltpu.SemaphoreType.DMA, pltpu.SemaphoreType.DMA],
        compiler_params=pltpu.CompilerParams(needs_layout_passes=False),
        name="dispatch")
    def k(cls_hbm, rank_hbm, ps_hbm, table_hbm, xs_hbm, inv_hbm,
          cls_v, ps_v, inv_v, idx_a, idx_b, rows_a, rows_b, rsem_a, rsem_b, wsem_a, wsem_b):
        base = _sc_worker_id() * tw
        pltpu.sync_copy(ps_hbm, ps_v)
        pltpu.sync_copy(cls_hbm.at[pl.ds(base, tw)], cls_v)
        pltpu.sync_copy(rank_hbm.at[pl.ds(base, tw)], inv_v)

        @pl.loop(0, tw // SC_LANES)
        def _(i):
            sl = pl.ds(i * SC_LANES, SC_LANES)
            inv_v[sl] = plsc.load_gather(ps_v, [cls_v[sl]]) + inv_v[sl]

        pltpu.sync_copy(inv_v, inv_hbm.at[pl.ds(base, tw)])

        slots = ((idx_a, rows_a, rsem_a, wsem_a), (idx_b, rows_b, rsem_b, wsem_b))

        def read(j, slot):
            _, rows, rsem, _ = slots[slot]
            return pltpu.make_async_copy(
                table_hbm.at[pl.ds(base + j * SC_ROWS, SC_ROWS)], rows, rsem)

        def write(slot):
            idx, rows, _, wsem = slots[slot]
            return pltpu.make_async_copy(rows, xs_hbm.at[idx], wsem)

        @pl.loop(0, n_chunks // 2)
        def _(jj):
            for slot in range(2):
                j = 2 * jj + slot
                read(j, slot).start()
                for q in range(SC_ROWS // SC_LANES):
                    slots[slot][0][pl.ds(q * SC_LANES, SC_LANES)] = (
                        inv_v[pl.ds(j * SC_ROWS + q * SC_LANES, SC_LANES)])
            for slot in range(2):
                read(2 * jj + slot, slot).wait()
                write(slot).start()
            for slot in range(2):
                write(slot).wait()

    return k(cls, rank, pstart, table)


def _sc_unsort(table, inv):
    t = inv.shape[0]
    width = table.shape[1]
    tw = t // SC_WORKERS
    n_chunks = tw // SC_ROWS
    assert t % (SC_WORKERS * 2 * SC_ROWS) == 0
    mesh = plsc.VectorSubcoreMesh(core_axis_name="c", subcore_axis_name="s")

    @functools.partial(
        pl.kernel, mesh=mesh, out_type=jax.ShapeDtypeStruct((t, width), I32),
        scratch_types=_sc_scratch(tw, width),
        compiler_params=pltpu.CompilerParams(needs_layout_passes=False),
        name="unsort")
    def k(table_hbm, inv_hbm, out_hbm, idx_all, idx_a, idx_b, rows_a, rows_b, sem_a, sem_b):
        base = _sc_worker_id() * tw
        pltpu.sync_copy(inv_hbm.at[pl.ds(base, tw)], idx_all)
        _sc_gather_rows(table_hbm, out_hbm, idx_all, base, n_chunks,
                        (idx_a, idx_b), (rows_a, rows_b), (sem_a, sem_b))

    return k(table, inv)


def _combine_kernel(x_ref, y_ref, mod_ref, gf_ref, o_ref, *, final):
    xn = x_ref[0] + mod_ref[0][5:6] * _unpack_pairs(y_ref[0])
    if final:
        xn = _rms(xn) * gf_ref[...]
    o_ref[0] = xn


def _combine(x, y, mod, g_final, final):
    b, seq, d = x.shape
    tm = TM_COMBINE
    blk = pl.BlockSpec((1, tm, d), lambda bi, i: (bi, i, 0))
    return pl.pallas_call(
        functools.partial(_combine_kernel, final=final),
        grid=(b, seq // tm),
        in_specs=[blk, pl.BlockSpec((1, tm, W_PACK), lambda bi, i: (bi, i, 0)),
                  pl.BlockSpec((1, N_MOD, d), lambda bi, i: (bi, 0, 0)),
                  pl.BlockSpec((1, d), lambda bi, i: (0, 0))],
        out_specs=blk,
        out_shape=jax.ShapeDtypeStruct((b, seq, d), F32),
        compiler_params=pltpu.CompilerParams(
            dimension_semantics=("arbitrary", "arbitrary"), vmem_limit_bytes=VMEM_LIMIT),
        name="combine",
    )(x, y, mod, g_final)


def _dft_tables(seq):
    n = np.arange(seq)
    ang = 2.0 * np.pi * ((n[:, None] * n[None, :]) % seq) / seq
    scale = 1.0 / np.sqrt(seq)
    return (np.cos(ang) * scale).astype(np.float32), (-np.sin(ang) * scale).astype(np.float32)


def _dft64_blockdiag():
    n = np.arange(HEAD_DIM)
    ang = 2.0 * np.pi * ((n[:, None] * n[None, :]) % HEAD_DIM) / HEAD_DIM
    c, s = np.cos(ang) / np.sqrt(HEAD_DIM), np.sin(ang) / np.sqrt(HEAD_DIM)
    bd = np.zeros((GROUP, 2 * GROUP), np.float32)
    for h in range(N_HEADS):
        r = slice(h * HEAD_DIM, (h + 1) * HEAD_DIM)
        bd[r, h * HEAD_DIM:(h + 1) * HEAD_DIM] = c
        bd[r, GROUP + h * HEAD_DIM:GROUP + (h + 1) * HEAD_DIM] = s
    return bd


def _block_diag(w):
    n, k, _ = w.shape
    eye = jnp.eye(n, dtype=w.dtype)
    return (eye[:, None, :, None] * w[:, :, None, :]).reshape(n * k, n * k)


def _route_tables(counts, t, tm):
    nt = t // tm + N_CLASSES
    nt += -nt % 8
    ntile_c = (counts + tm - 1) // tm
    tile_end = jnp.cumsum(ntile_c)
    pstart = (tile_end - ntile_c) * tm
    n_used = tile_end[-1]
    tile_ids = jnp.minimum(jnp.arange(nt, dtype=jnp.int32), n_used - 1)
    tile_cls = jnp.sum((tile_ids[:, None] >= tile_end[None, :]).astype(jnp.int32), axis=1)
    pair_lo = jnp.array([p[0] for p in PAIRS], jnp.int32)
    pair_hi = jnp.array([p[1] for p in PAIRS], jnp.int32)
    grp, pid = tile_cls // len(PAIRS), tile_cls % len(PAIRS)
    tile_ea = grp * EXPERTS_PER_GROUP + jnp.take(pair_lo, pid)
    tile_eb = grp * EXPERTS_PER_GROUP + jnp.take(pair_hi, pid)
    pstart = jnp.concatenate([pstart, jnp.zeros((CLS_PAD - N_CLASSES,), I32)])
    return pstart, tile_ea, tile_eb, n_used.reshape(1).astype(jnp.int32), nt


def _trunk(x, mods, lw, shared):
    b, seq, d = x.shape
    t = b * seq
    depth = len(lw)
    prev = None
    for l in range(depth):
        w = lw[l]
        mod = mods[l]
        xg, part = _mix_in(x, prev, mod, w["g1"], w["w_in"], shared["d64"], w["ws"], w["bs"],
                           w["wc"], w["wp"], w["ps"], w["gg"], w["wo_bcd"])
        x_mid, h2, cls, rank, cnt = _mix_out(
            shared["cs"], shared["ss"], xg, part, x, prev, mod, w["wf"], w["gg"], w["wo_a"],
            w["g2"], shared["wr_both"], shared["wr_hi"], shared["b_router"], shared["tri"])
        counts = cnt[:N_CLASSES, 0].astype(I32)
        pstart, tile_ea, tile_eb, n_used, nt = _route_tables(counts, t, TM_MOE)
        xs, inv = _sc_dispatch(cls.reshape(t), rank.reshape(t), pstart, h2.reshape(t, W_ROW),
                               nt * TM_MOE)
        ys = _moe_ffn(tile_ea + l * N_EXPERTS, tile_eb + l * N_EXPERTS, n_used, xs,
                      shared["wg"], shared["wu"], shared["wd"])
        y_tok = _sc_unsort(ys, inv).reshape(b, seq, W_PACK)
        x, prev = x_mid, (y_tok, mod)
    return _combine(x, prev[0], prev[1], shared["g_final"], final=True)


def kernel(x_prompt, x_sample, c_prompt, c_sample, w_ada, b_ada, g_norm1, w_in, w_fourier, w_spatial, b_spatial, w_conv, w_pool, pool_scale, g_group, w_out, g_norm2, w_router, b_router, w_exp_gate, w_exp_up, w_exp_down, g_final):
    depth = w_in.shape[0]
    seq = x_prompt.shape[1]
    d = D_MODEL
    nb_p = c_prompt.shape[0]

    cs, ss = _dft_tables(seq)
    w_router = jnp.pad(w_router, ((0, 0), (0, 128 - N_EXPERTS)))
    wr_hi = w_router.astype(BF16)
    wr_lo = (w_router - wr_hi.astype(F32)).astype(BF16)
    shared = {
        "cs": jnp.asarray(cs).astype(BF16),
        "ss": jnp.asarray(ss).astype(BF16),
        "d64": jnp.asarray(_dft64_blockdiag()).astype(BF16),
        "wr_both": jnp.concatenate([wr_hi, wr_lo], axis=1), "wr_hi": wr_hi,
        "b_router": b_router.reshape(N_EXPERTS, 1).astype(F32),
        "g_final": g_final.reshape(1, d),
        "tri": jnp.asarray(np.triu(np.ones((TM_MIX, TM_MIX), np.float32))).astype(BF16),
    }
    lw = []
    for l in range(depth):
        lw.append({
            "g1": g_norm1[l].reshape(1, d),
            "w_in": w_in[l].astype(BF16),
            "ws": w_spatial[l].reshape(N_HEADS * CHUNK, CHUNK).astype(BF16),
            "bs": jnp.repeat(b_spatial[l].T, HEAD_DIM, axis=1),
            "wc": w_conv[l],
            "wp": _block_diag(w_pool[l]).astype(BF16),
            "ps": pool_scale[l].reshape(1, GROUP),
            "gg": g_group[l].reshape(1, d),
            "wo_bcd": w_out[l, GROUP:].astype(BF16),
            "wo_a": w_out[l, :GROUP].astype(BF16),
            "wf": _block_diag(w_fourier[l]).astype(BF16),
            "g2": g_norm2[l].reshape(1, d),
        })
    shared["wg"] = w_exp_gate.reshape(depth * N_EXPERTS, d, D_EXPERT)
    shared["wu"] = w_exp_up.reshape(depth * N_EXPERTS, d, D_EXPERT)
    shared["wd"] = w_exp_down.reshape(depth * N_EXPERTS, D_EXPERT, d)

    c_all = jnp.concatenate([c_prompt, c_sample], axis=0)
    mod_all = _ada(c_all, w_ada, b_ada).reshape(depth, c_all.shape[0], N_MOD, d)
    mods_p = [mod_all[l, :nb_p] for l in range(depth)]
    mods_s = [mod_all[l, nb_p:] for l in range(depth)]
    y_prompt = _trunk(x_prompt, mods_p, lw, shared)
    y_sample = _trunk(x_sample, mods_s, lw, shared)
    return (y_prompt, y_sample)
```

```python
import functools

import numpy as np
import jax
import jax.numpy as jnp
from jax import lax
from jax.experimental import pallas as pl
from jax.experimental.pallas import tpu as pltpu
from jax.experimental.pallas import tpu_sc as plsc

F32 = jnp.float32
BF16 = jnp.bfloat16
I32 = jnp.int32
U32 = jnp.uint32

D_MODEL = 1024
GROUP = 256
N_HEADS = 4
HEAD_DIM = 64
CHUNK = 128
D_IN = 7 * GROUP
N_EXPERTS = 16
N_EGROUPS = 4
EXPERTS_PER_GROUP = 4
D_EXPERT = 512
N_MOD = 6
EPS = 1e-6
HALO = 8
PAIRS = ((0, 1), (0, 2), (0, 3), (1, 2), (1, 3), (2, 3))
N_CLASSES = N_EGROUPS * len(PAIRS)
CLS_PAD = 32

TM_IN = 512
TM_MIX = 512
TM_MOE = 512
SEQ_PER_STEP = 2
TM_COMBINE = 1024
VMEM_LIMIT = 56 * 1024 * 1024

W_PACK = D_MODEL // 2
W_EXTRA = 128
W_ROW = W_PACK + W_EXTRA

SC_CORES = 2
SC_SUBCORES = 16
SC_LANES = 16
SC_WORKERS = SC_CORES * SC_SUBCORES
SC_ROWS = 64
SC_SCAN = 2048


def _rms(x):
    return x * lax.rsqrt(jnp.mean(x * x, axis=-1, keepdims=True) + EPS)


def _dot(a, b):
    return jnp.dot(a, b, preferred_element_type=F32)


def _pack_pairs(x):
    k = x.shape[1] // 2
    xb = x.astype(BF16).astype(F32)
    hi = lax.bitcast_convert_type(xb[:, :k], U32)
    lo = lax.bitcast_convert_type(xb[:, k:], U32)
    return lax.bitcast_convert_type(hi | (lo >> 16), I32)


def _unpack_pairs(w):
    u = lax.bitcast_convert_type(w, U32)
    hi = lax.bitcast_convert_type(u & jnp.uint32(0xFFFF0000), F32)
    lo = lax.bitcast_convert_type(u << 16, F32)
    return jnp.concatenate([hi, lo], axis=1)


def _ada_kernel(c_ref, w_ref, b_ref, o_ref):
    c = c_ref[...]
    sc = c * jax.nn.sigmoid(c)
    o_ref[0] = _dot(sc.astype(BF16), w_ref[0].astype(BF16)) + b_ref[0]


def _ada(c, w_ada, b_ada):
    depth, d, n = w_ada.shape
    bc = c.shape[0]
    tn = 1536
    return pl.pallas_call(
        _ada_kernel,
        grid=(depth, n // tn),
        in_specs=[
            pl.BlockSpec((bc, d), lambda l, j: (0, 0)),
            pl.BlockSpec((1, d, tn), lambda l, j: (l, 0, j)),
            pl.BlockSpec((1, 1, tn), lambda l, j: (l, 0, j)),
        ],
        out_specs=pl.BlockSpec((1, bc, tn), lambda l, j: (l, 0, j)),
        out_shape=jax.ShapeDtypeStruct((depth, bc, n), F32),
        compiler_params=pltpu.CompilerParams(
            dimension_semantics=("arbitrary", "arbitrary"), vmem_limit_bytes=VMEM_LIMIT),
        name="ada_mod",
    )(c, w_ada, b_ada.reshape(depth, 1, n))


def _mix_in_kernel(*refs, tm, seq, fused):
    if fused:
        ym_ref, yp_ref, yn_ref, modp_ref = refs[3:7]
        refs = refs[:3] + refs[7:]
    (xm_ref, xp_ref, xn_ref, mod_ref, g1_ref, win_ref, d64_ref, ws_ref, bs_ref,
     wc_ref, wp_ref, ps_ref, gg_ref, wo_ref, xg_ref, part_ref) = refs
    i = pl.program_id(1)
    ne = tm + 2 * HALO
    xe = jnp.concatenate([xp_ref[0], xm_ref[0], xn_ref[0]], axis=0)
    if fused:
        ye = jnp.concatenate([yp_ref[0], ym_ref[0], yn_ref[0]], axis=0)
        xe = xe + modp_ref[0][5:6] * _unpack_pairs(ye)
    mod = mod_ref[0]
    h = _rms(xe) * g1_ref[...] * (1.0 + mod[1:2]) + mod[0:1]
    proj = _dot(h.astype(BF16), win_ref[...])

    gpos = lax.broadcasted_iota(jnp.int32, (ne, GROUP), 0) + (i * tm - HALO)
    valid = (gpos >= 0) & (gpos < seq)
    pm = proj[HALO:HALO + tm]

    xg_ref[0] = _dot(pm[:, 0:GROUP].astype(BF16), d64_ref[...]).astype(BF16)

    u = pm[:, GROUP:2 * GROUP]
    v = pm[:, 2 * GROUP:3 * GROUP]
    head = lax.broadcasted_iota(jnp.int32, (CHUNK, GROUP), 1) // HEAD_DIM
    yb_chunks = []
    for c in range(tm // CHUNK):
        vc = v[c * CHUNK:(c + 1) * CHUNK].astype(BF16)
        m_all = _dot(ws_ref[...], vc)
        mixed = bs_ref[...]
        for hh in range(N_HEADS):
            mixed = mixed + jnp.where(head == hh, m_all[hh * CHUNK:(hh + 1) * CHUNK], 0.0)
        yb_chunks.append(u[c * CHUNK:(c + 1) * CHUNK] * mixed)
    yb = jnp.concatenate(yb_chunks, axis=0)

    z = jnp.where(valid, proj[:, 4 * GROUP:5 * GROUP] * proj[:, 5 * GROUP:6 * GROUP], 0.0)
    conv = (pltpu.roll(z, 1, 0)[HALO:HALO + tm] * wc_ref[0:1, :]
            + z[HALO:HALO + tm] * wc_ref[1:2, :]
            + pltpu.roll(z, ne - 1, 0)[HALO:HALO + tm] * wc_ref[2:3, :])
    yc = pm[:, 3 * GROUP:4 * GROUP] * conv

    p = jnp.where(valid, proj[:, 6 * GROUP:7 * GROUP], 0.0)
    a2 = p + pltpu.roll(p, 1, 0)
    a4 = a2 + pltpu.roll(a2, 2, 0)
    a8 = a4 + pltpu.roll(a4, 4, 0)
    a16 = a8 + pltpu.roll(a8, 8, 0)
    w2 = a2[HALO:HALO + tm]
    w4 = pltpu.roll(a4, ne - 1, 0)[HALO:HALO + tm]
    w8 = pltpu.roll(a8, ne - 3, 0)[HALO:HALO + tm]
    w16 = pltpu.roll(a16, ne - 7, 0)[HALO:HALO + tm]
    grp = lax.broadcasted_iota(jnp.int32, (tm, GROUP), 1) // HEAD_DIM
    wsum = jnp.where(grp == 0, w2, jnp.where(grp == 1, w4, jnp.where(grp == 2, w8, w16)))
    left = jnp.left_shift(1, grp)
    t = lax.broadcasted_iota(jnp.int32, (tm, GROUP), 0) + i * tm
    cnt = jnp.minimum(t + left, seq) - jnp.maximum(t - left, 0)
    pooled = wsum / cnt.astype(F32) - p[HALO:HALO + tm]
    yd = _dot(pooled.astype(BF16), wp_ref[...]) * ps_ref[...]

    gg = gg_ref[...]
    ycat = jnp.concatenate([
        (_rms(yb) * gg[:, GROUP:2 * GROUP]).astype(BF16),
        (_rms(yc) * gg[:, 2 * GROUP:3 * GROUP]).astype(BF16),
        (_rms(yd) * gg[:, 3 * GROUP:4 * GROUP]).astype(BF16)], axis=1)
    part_ref[0] = _dot(ycat, wo_ref[...]).astype(part_ref.dtype)


def _mix_in(x, prev, mod, g1, w_in, d64, ws, bs, wc, wp, ps, gg, wo_bcd):
    b, seq, d = x.shape
    tm = TM_IN
    nt = seq // tm
    hb = tm // HALO
    full = lambda *shape: pl.BlockSpec(shape, lambda bi, i: (0,) * len(shape))
    mod_spec = pl.BlockSpec((1, N_MOD, d), lambda bi, i: (bi, 0, 0))

    def rows(width):
        return [
            pl.BlockSpec((1, tm, width), lambda bi, i: (bi, i, 0)),
            pl.BlockSpec((1, HALO, width), lambda bi, i: (bi, jnp.maximum(i * hb - 1, 0), 0)),
            pl.BlockSpec((1, HALO, width),
                         lambda bi, i: (bi, jnp.minimum((i + 1) * hb, seq // HALO - 1), 0)),
        ]

    fused = prev is not None
    prev_specs = rows(W_PACK) + [mod_spec] if fused else []
    prev_args = (prev[0], prev[0], prev[0], prev[1]) if fused else ()
    return pl.pallas_call(
        functools.partial(_mix_in_kernel, tm=tm, seq=seq, fused=fused),
        grid=(b, nt),
        in_specs=rows(d) + prev_specs + [
            mod_spec,
            full(1, d), full(d, D_IN), full(GROUP, 2 * GROUP), full(N_HEADS * CHUNK, CHUNK),
            full(CHUNK, GROUP), full(3, GROUP), full(GROUP, GROUP), full(1, GROUP),
            full(1, d), full(3 * GROUP, d),
        ],
        out_specs=[
            pl.BlockSpec((1, tm, 2 * GROUP), lambda bi, i: (bi, i, 0)),
            pl.BlockSpec((1, tm, d), lambda bi, i: (bi, i, 0)),
        ],
        out_shape=[
            jax.ShapeDtypeStruct((b, seq, 2 * GROUP), BF16),
            jax.ShapeDtypeStruct((b, seq, d), BF16),
        ],
        compiler_params=pltpu.CompilerParams(
            dimension_semantics=("arbitrary", "arbitrary"), vmem_limit_bytes=VMEM_LIMIT),
        name="mix_in",
    )(x, x, x, *prev_args, mod, g1, w_in, d64, ws, bs, wc, wp, ps, gg, wo_bcd)


def _top2_of4(vals, aux):
    best, bi, ba = vals[0], jnp.zeros_like(vals[0], jnp.int32), aux[0]
    for j in range(1, 4):
        gt = vals[j] > best
        best = jnp.where(gt, vals[j], best)
        bi = jnp.where(gt, j, bi)
        ba = jnp.where(gt, aux[j], ba)
    sec = jnp.full_like(best, -jnp.inf)
    si, sa = jnp.zeros_like(bi), aux[0]
    for j in range(4):
        cand = jnp.where(bi == j, -jnp.inf, vals[j])
        gt = cand > sec
        sec = jnp.where(gt, cand, sec)
        si = jnp.where(gt, j, si)
        sa = jnp.where(gt, aux[j], sa)
    return best, sec, bi, si, ba, sa


def _mix_out_kernel(*refs, fused):
    y_ref = modp_ref = None
    if fused:
        y_ref, modp_ref = refs[5:7]
        refs = refs[:5] + refs[7:]
    (cs_ref, ss_ref, xg_ref, part_ref, x_ref, mod_ref, wf_ref, gg_ref, wo_ref,
     g2_ref, wrh_ref, wrl_ref, br_ref, tri_ref, xo_ref, h2_ref, cls_ref, rk_ref,
     cnt_ref, carry_ref) = refs

    @pl.when((pl.program_id(0) == 0) & (pl.program_id(1) == 0))
    def _():
        carry_ref[...] = jnp.zeros_like(carry_ref)

    nq = xg_ref.shape[0]
    xc = jnp.concatenate([xg_ref[q, :, 0:GROUP] for q in range(nq)], axis=1)
    xs = jnp.concatenate([xg_ref[q, :, GROUP:2 * GROUP] for q in range(nq)], axis=1)
    f_all = _dot(cs_ref[...], xc) + _dot(ss_ref[...], xs)
    for q in range(nq):
        x_in = x_ref[q]
        if fused:
            x_in = x_in + modp_ref[q][5:6] * _unpack_pairs(y_ref[q])
        _mix_out_tokens(q, f_all[:, q * GROUP:(q + 1) * GROUP], x_in, part_ref, mod_ref, wf_ref,
                        gg_ref, wo_ref, g2_ref, wrh_ref, wrl_ref, br_ref, tri_ref, xo_ref, h2_ref,
                        cls_ref, rk_ref, carry_ref)
    cnt_ref[...] = carry_ref[...]


def _mix_out_tokens(q, f, x_in, part_ref, mod_ref, wf_ref, gg_ref, wo_ref, g2_ref, wrh_ref,
                    wrl_ref, br_ref, tri_ref, xo_ref, h2_ref, cls_ref, rk_ref, carry_ref):
    ya = _dot(f.astype(BF16), wf_ref[...])
    ya = _rms(ya) * gg_ref[:, 0:GROUP]
    mix = _dot(ya.astype(BF16), wo_ref[...]) + part_ref[q].astype(F32)
    mod = mod_ref[q]
    xn = x_in + mod[2:3] * mix
    xo_ref[q] = xn
    h2 = _rms(xn) * (g2_ref[...] * (1.0 + mod[4:5])) + mod[3:4]
    h_hi = h2.astype(BF16)
    h2_ref[q, :, 0:W_PACK] = _pack_pairs(h2)
    h_lo = (h2 - h_hi.astype(F32)).astype(BF16)

    both = _dot(h_hi, wrh_ref[...])
    logits = both[:, 0:128] + both[:, 128:256] + _dot(h_lo, wrl_ref[...])
    lt = logits.T[0:N_EXPERTS]
    ex = jnp.exp(lt - jnp.max(lt, axis=0, keepdims=True))
    scores = ex / jnp.sum(ex, axis=0, keepdims=True)
    biased = scores + br_ref[...]

    best_gs = None
    for g in range(N_EGROUPS):
        rows = [biased[g * 4 + j:g * 4 + j + 1] for j in range(4)]
        srow = [scores[g * 4 + j:g * 4 + j + 1] for j in range(4)]
        t1, t2, i1, i2, s1, s2 = _top2_of4(rows, srow)
        gs = t1 + t2
        if best_gs is None:
            best_gs, sel = gs, jnp.zeros_like(i1)
            bi1, bi2, bs1, bs2 = i1, i2, s1, s2
        else:
            gt = gs > best_gs
            best_gs = jnp.where(gt, gs, best_gs)
            sel = jnp.where(gt, g, sel)
            bi1, bi2 = jnp.where(gt, i1, bi1), jnp.where(gt, i2, bi2)
            bs1, bs2 = jnp.where(gt, s1, bs1), jnp.where(gt, s2, bs2)
    first_lo = bi1 < bi2
    lo = jnp.where(first_lo, bi1, bi2)
    hi = jnp.where(first_lo, bi2, bi1)
    den = bs1 + bs2
    w_a = jnp.where(first_lo, bs1, bs2) / den
    w_b = jnp.where(first_lo, bs2, bs1) / den
    pair = jnp.where(lo == 0, hi - 1, jnp.where(lo == 1, hi + 1, 5))
    cls_i = sel * len(PAIRS) + pair
    tk = cls_i.shape[1]
    gate_cols = jnp.concatenate([w_a, w_b, jnp.zeros((W_EXTRA - 2, tk), F32)], axis=0).T
    h2_ref[q, :, W_PACK:W_ROW] = lax.bitcast_convert_type(gate_cols, I32)

    onehot = lax.broadcasted_iota(jnp.int32, (CLS_PAD, tk), 0) == cls_i
    prefix = _dot(jnp.where(onehot, 1.0, 0.0).astype(BF16), tri_ref[...])
    carry = carry_ref[...]
    rank = jnp.sum(jnp.where(onehot, prefix - 1.0 + carry[:, 0:1], 0.0), axis=0, keepdims=True)
    carry_ref[...] = carry + prefix[:, tk - 1:tk]
    cls_ref[q] = cls_i
    rk_ref[q] = rank.astype(I32)


def _mix_out(cs, ss, xg, part, x, prev, mod, wf_bd, gg, wo_a, g2, wr_hi, wr_lo, b_router, tri):
    b, seq, d = x.shape
    tk = TM_MIX
    nk = seq // tk
    nq = SEQ_PER_STEP
    full = lambda *shape: pl.BlockSpec(shape, lambda k, bi: (0,) * len(shape))
    rows = lambda width: pl.BlockSpec((nq, tk, width), lambda k, bi: (bi, k, 0))
    lanes = pl.BlockSpec((nq, 1, tk), lambda k, bi: (bi, 0, k))
    mod_spec = pl.BlockSpec((nq, N_MOD, d), lambda k, bi: (bi, 0, 0))
    fused = prev is not None
    prev_specs = [rows(W_PACK), mod_spec] if fused else []
    prev_args = tuple(prev) if fused else ()
    return pl.pallas_call(
        functools.partial(_mix_out_kernel, fused=fused),
        grid=(nk, b // nq),
        in_specs=[
            pl.BlockSpec((tk, seq), lambda k, bi: (k, 0)),
            pl.BlockSpec((tk, seq), lambda k, bi: (k, 0)),
            pl.BlockSpec((nq, seq, 2 * GROUP), lambda k, bi: (bi, 0, 0)),
            rows(d), rows(d),
        ] + prev_specs + [
            mod_spec,
            full(GROUP, GROUP), full(1, d), full(GROUP, d), full(1, d),
            full(d, 256), full(d, 128), full(N_EXPERTS, 1), full(tk, tk),
        ],
        out_specs=[rows(d), rows(W_ROW), lanes, lanes, full(CLS_PAD, 128)],
        out_shape=[
            jax.ShapeDtypeStruct((b, seq, d), F32),
            jax.ShapeDtypeStruct((b, seq, W_ROW), I32),
            jax.ShapeDtypeStruct((b, 1, seq), I32),
            jax.ShapeDtypeStruct((b, 1, seq), I32),
            jax.ShapeDtypeStruct((CLS_PAD, 128), F32),
        ],
        scratch_shapes=[pltpu.VMEM((CLS_PAD, 128), F32)],
        compiler_params=pltpu.CompilerParams(
            dimension_semantics=("arbitrary", "arbitrary"), vmem_limit_bytes=VMEM_LIMIT),
        name="mix_out",
    )(cs, ss, xg, part, x, *prev_args, mod, wf_bd, gg, wo_a, g2, wr_hi, wr_lo, b_router, tri)


def _moe_kernel(ea_ref, eb_ref, nu_ref, xs_ref, wga_ref, wua_ref, wda_ref,
                wgb_ref, wub_ref, wdb_ref, o_ref):
    @pl.when(pl.program_id(0) < nu_ref[0])
    def _():
        xs = _unpack_pairs(xs_ref[:, 0:W_PACK]).astype(BF16)
        wt = lax.bitcast_convert_type(xs_ref[:, W_PACK:W_ROW], F32)

        def ffn(wg, wu, wd):
            a = _dot(xs, wg[0])
            hid = (a * jax.nn.sigmoid(a)) * _dot(xs, wu[0])
            return _dot(hid.astype(BF16), wd[0])

        o_ref[...] = _pack_pairs(wt[:, 0:1] * ffn(wga_ref, wua_ref, wda_ref)
                                 + wt[:, 1:2] * ffn(wgb_ref, wub_ref, wdb_ref))


def _moe_ffn(tile_ea, tile_eb, n_used, xs, wg, wu, wd):
    p = xs.shape[0]
    d = D_MODEL
    tm = TM_MOE
    nt = p // tm
    row = lambda i, ea, eb, nu: (jnp.minimum(i, nu[0] - 1), 0)
    exp_a = lambda i, ea, eb, nu: (ea[i], 0, 0)
    exp_b = lambda i, ea, eb, nu: (eb[i], 0, 0)
    return pl.pallas_call(
        _moe_kernel,
        grid_spec=pltpu.PrefetchScalarGridSpec(
            num_scalar_prefetch=3,
            grid=(nt,),
            in_specs=[
                pl.BlockSpec((tm, W_ROW), row),
                pl.BlockSpec((1, d, D_EXPERT), exp_a),
                pl.BlockSpec((1, d, D_EXPERT), exp_a),
                pl.BlockSpec((1, D_EXPERT, d), exp_a),
                pl.BlockSpec((1, d, D_EXPERT), exp_b),
                pl.BlockSpec((1, d, D_EXPERT), exp_b),
                pl.BlockSpec((1, D_EXPERT, d), exp_b),
            ],
            out_specs=pl.BlockSpec((tm, W_PACK), row),
        ),
        out_shape=jax.ShapeDtypeStruct((p, W_PACK), I32),
        compiler_params=pltpu.CompilerParams(
            dimension_semantics=("arbitrary",), vmem_limit_bytes=VMEM_LIMIT),
        name="moe_ffn",
    )(tile_ea, tile_eb, n_used, xs, wg, wu, wd, wg, wu, wd)


def _sc_worker_id():
    return lax.axis_index("s") * SC_CORES + lax.axis_index("c")


def _sc_gather_rows(table_hbm, out_hbm, idx_all, out_base, n_chunks, idx_bufs, row_bufs, sems):
    def copy(slot):
        return pltpu.make_async_copy(table_hbm.at[idx_bufs[slot]], row_bufs[slot], sems[slot])

    def start(j, slot):
        for q in range(SC_ROWS // SC_LANES):
            idx_bufs[slot][pl.ds(q * SC_LANES, SC_LANES)] = (
                idx_all[pl.ds(j * SC_ROWS + q * SC_LANES, SC_LANES)])
        copy(slot).start()

    def flush(j, slot):
        copy(slot).wait()
        pltpu.sync_copy(row_bufs[slot], out_hbm.at[pl.ds(out_base + j * SC_ROWS, SC_ROWS)])

    start(0, 0)

    @pl.loop(0, n_chunks // 2)
    def _(jj):
        j = 2 * jj
        start(j + 1, 1)
        flush(j, 0)

        @pl.when(j + 2 < n_chunks)
        def _():
            start(j + 2, 0)

        flush(j + 1, 1)


def _sc_scratch(n_idx, width):
    return [
        pltpu.VMEM((n_idx,), I32),
        pltpu.VMEM((SC_ROWS,), I32), pltpu.VMEM((SC_ROWS,), I32),
        pltpu.VMEM((SC_ROWS, width), I32), pltpu.VMEM((SC_ROWS, width), I32),
        pltpu.SemaphoreType.DMA, pltpu.SemaphoreType.DMA,
    ]


def _sc_dispatch(cls, rank, pstart, table, p):
    t, width = table.shape
    tw = t // SC_WORKERS
    n_chunks = tw // SC_ROWS
    assert t % (SC_WORKERS * 2 * SC_ROWS) == 0
    mesh = plsc.VectorSubcoreMesh(core_axis_name="c", subcore_axis_name="s")

    @functools.partial(
        pl.kernel, mesh=mesh,
        out_type=[jax.ShapeDtypeStruct((p, width), I32), jax.ShapeDtypeStruct((t,), I32)],
        scratch_types=[pltpu.VMEM((tw,), I32), pltpu.VMEM((CLS_PAD,), I32)]
        + _sc_scratch(tw, width) + [pltpu.SemaphoreType.DMA, pltpu.SemaphoreType.DMA],
        compiler_params=pltpu.CompilerParams(needs_layout_passes=False),
        name="dispatch")
    def k(cls_hbm, rank_hbm, ps_hbm, table_hbm, xs_hbm, inv_hbm,
          cls_v, ps_v, inv_v, idx_a, idx_b, rows_a, rows_b, rsem_a, rsem_b, wsem_a, wsem_b):
        base = _sc_worker_id() * tw
        pltpu.sync_copy(ps_hbm, ps_v)
        pltpu.sync_copy(cls_hbm.at[pl.ds(base, tw)], cls_v)
        pltpu.sync_copy(rank_hbm.at[pl.ds(base, tw)], inv_v)

        @pl.loop(0, tw // SC_LANES)
        def _(i):
            sl = pl.ds(i * SC_LANES, SC_LANES)
            inv_v[sl] = plsc.load_gather(ps_v, [cls_v[sl]]) + inv_v[sl]

        pltpu.sync_copy(inv_v, inv_hbm.at[pl.ds(base, tw)])

        slots = ((idx_a, rows_a, rsem_a, wsem_a), (idx_b, rows_b, rsem_b, wsem_b))

        def read(j, slot):
            _, rows, rsem, _ = slots[slot]
            return pltpu.make_async_copy(
                table_hbm.at[pl.ds(base + j * SC_ROWS, SC_ROWS)], rows, rsem)

        def write(slot):
            idx, rows, _, wsem = slots[slot]
            return pltpu.make_async_copy(rows, xs_hbm.at[idx], wsem)

        @pl.loop(0, n_chunks // 2)
        def _(jj):
            for slot in range(2):
                j = 2 * jj + slot
                read(j, slot).start()
                for q in range(SC_ROWS // SC_LANES):
                    slots[slot][0][pl.ds(q * SC_LANES, SC_LANES)] = (
                        inv_v[pl.ds(j * SC_ROWS + q * SC_LANES, SC_LANES)])
            for slot in range(2):
                read(2 * jj + slot, slot).wait()
                write(slot).start()
            for slot in range(2):
                write(slot).wait()

    return k(cls, rank, pstart, table)


def _sc_unsort(table, inv):
    t = inv.shape[0]
    width = table.shape[1]
    tw = t // SC_WORKERS
    n_chunks = tw // SC_ROWS
    assert t % (SC_WORKERS * 2 * SC_ROWS) == 0
    mesh = plsc.VectorSubcoreMesh(core_axis_name="c", subcore_axis_name="s")

    @functools.partial(
        pl.kernel, mesh=mesh, out_type=jax.ShapeDtypeStruct((t, width), I32),
        scratch_types=_sc_scratch(tw, width),
        compiler_params=pltpu.CompilerParams(needs_layout_passes=False),
        name="unsort")
    def k(table_hbm, inv_hbm, out_hbm, idx_all, idx_a, idx_b, rows_a, rows_b, sem_a, sem_b):
        base = _sc_worker_id() * tw
        pltpu.sync_copy(inv_hbm.at[pl.ds(base, tw)], idx_all)
        _sc_gather_rows(table_hbm, out_hbm, idx_all, base, n_chunks,
                        (idx_a, idx_b), (rows_a, rows_b), (sem_a, sem_b))

    return k(table, inv)


def _combine_kernel(x_ref, y_ref, mod_ref, gf_ref, o_ref, *, final):
    xn = x_ref[0] + mod_ref[0][5:6] * _unpack_pairs(y_ref[0])
    if final:
        xn = _rms(xn) * gf_ref[...]
    o_ref[0] = xn


def _combine(x, y, mod, g_final, final):
    b, seq, d = x.shape
    tm = TM_COMBINE
    blk = pl.BlockSpec((1, tm, d), lambda bi, i: (bi, i, 0))
    return pl.pallas_call(
        functools.partial(_combine_kernel, final=final),
        grid=(b, seq // tm),
        in_specs=[blk, pl.BlockSpec((1, tm, W_PACK), lambda bi, i: (bi, i, 0)),
                  pl.BlockSpec((1, N_MOD, d), lambda bi, i: (bi, 0, 0)),
                  pl.BlockSpec((1, d), lambda bi, i: (0, 0))],
        out_specs=blk,
        out_shape=jax.ShapeDtypeStruct((b, seq, d), F32),
        compiler_params=pltpu.CompilerParams(
            dimension_semantics=("arbitrary", "arbitrary"), vmem_limit_bytes=VMEM_LIMIT),
        name="combine",
    )(x, y, mod, g_final)


def _dft_tables(seq):
    n = np.arange(seq)
    ang = 2.0 * np.pi * ((n[:, None] * n[None, :]) % seq) / seq
    scale = 1.0 / np.sqrt(seq)
    return (np.cos(ang) * scale).astype(np.float32), (-np.sin(ang) * scale).astype(np.float32)


def _dft64_blockdiag():
    n = np.arange(HEAD_DIM)
    ang = 2.0 * np.pi * ((n[:, None] * n[None, :]) % HEAD_DIM) / HEAD_DIM
    c, s = np.cos(ang) / np.sqrt(HEAD_DIM), np.sin(ang) / np.sqrt(HEAD_DIM)
    bd = np.zeros((GROUP, 2 * GROUP), np.float32)
    for h in range(N_HEADS):
        r = slice(h * HEAD_DIM, (h + 1) * HEAD_DIM)
        bd[r, h * HEAD_DIM:(h + 1) * HEAD_DIM] = c
        bd[r, GROUP + h * HEAD_DIM:GROUP + (h + 1) * HEAD_DIM] = s
    return bd


def _block_diag(w):
    n, k, _ = w.shape
    eye = jnp.eye(n, dtype=w.dtype)
    return (eye[:, None, :, None] * w[:, :, None, :]).reshape(n * k, n * k)


def _route_tables(counts, t, tm):
    nt = t // tm + N_CLASSES
    nt += -nt % 8
    ntile_c = (counts + tm - 1) // tm
    tile_end = jnp.cumsum(ntile_c)
    pstart = (tile_end - ntile_c) * tm
    n_used = tile_end[-1]
    tile_ids = jnp.minimum(jnp.arange(nt, dtype=jnp.int32), n_used - 1)
    tile_cls = jnp.sum((tile_ids[:, None] >= tile_end[None, :]).astype(jnp.int32), axis=1)
    pair_lo = jnp.array([p[0] for p in PAIRS], jnp.int32)
    pair_hi = jnp.array([p[1] for p in PAIRS], jnp.int32)
    grp, pid = tile_cls // len(PAIRS), tile_cls % len(PAIRS)
    tile_ea = grp * EXPERTS_PER_GROUP + jnp.take(pair_lo, pid)
    tile_eb = grp * EXPERTS_PER_GROUP + jnp.take(pair_hi, pid)
    pstart = jnp.concatenate([pstart, jnp.zeros((CLS_PAD - N_CLASSES,), I32)])
    return pstart, tile_ea, tile_eb, n_used.reshape(1).astype(jnp.int32), nt


def _trunk(x, mods, lw, shared):
    b, seq, d = x.shape
    t = b * seq
    depth = len(lw)
    prev = None
    for l in range(depth):
        w = lw[l]
        mod = mods[l]
        xg, part = _mix_in(x, prev, mod, w["g1"], w["w_in"], shared["d64"], w["ws"], w["bs"],
                           w["wc"], w["wp"], w["ps"], w["gg"], w["wo_bcd"])
        x_mid, h2, cls, rank, cnt = _mix_out(
            shared["cs"], shared["ss"], xg, part, x, prev, mod, w["wf"], w["gg"], w["wo_a"],
            w["g2"], shared["wr_both"], shared["wr_hi"], shared["b_router"], shared["tri"])
        counts = cnt[:N_CLASSES, 0].astype(I32)
        pstart, tile_ea, tile_eb, n_used, nt = _route_tables(counts, t, TM_MOE)
        xs, inv = _sc_dispatch(cls.reshape(t), rank.reshape(t), pstart, h2.reshape(t, W_ROW),
                               nt * TM_MOE)
        ys = _moe_ffn(tile_ea + l * N_EXPERTS, tile_eb + l * N_EXPERTS, n_used, xs,
                      shared["wg"], shared["wu"], shared["wd"])
        y_tok = _sc_unsort(ys, inv).reshape(b, seq, W_PACK)
        x, prev = x_mid, (y_tok, mod)
    return _combine(x, prev[0], prev[1], shared["g_final"], final=True)


def kernel(x_prompt, x_sample, c_prompt, c_sample, w_ada, b_ada, g_norm1, w_in, w_fourier, w_spatial, b_spatial, w_conv, w_pool, pool_scale, g_group, w_out, g_norm2, w_router, b_router, w_exp_gate, w_exp_up, w_exp_down, g_final):
    depth = w_in.shape[0]
    seq = x_prompt.shape[1]
    d = D_MODEL
    nb_p = c_prompt.shape[0]

    cs, ss = _dft_tables(seq)
    w_router = jnp.pad(w_router, ((0, 0), (0, 128 - N_EXPERTS)))
    wr_hi = w_router.astype(BF16)
    wr_lo = (w_router - wr_hi.astype(F32)).astype(BF16)
    shared = {
        "cs": jnp.asarray(cs).astype(BF16),
        "ss": jnp.asarray(ss).astype(BF16),
        "d64": jnp.asarray(_dft64_blockdiag()).astype(BF16),
        "wr_both": jnp.concatenate([wr_hi, wr_lo], axis=1), "wr_hi": wr_hi,
        "b_router": b_router.reshape(N_EXPERTS, 1).astype(F32),
        "g_final": g_final.reshape(1, d),
        "tri": jnp.asarray(np.triu(np.ones((TM_MIX, TM_MIX), np.float32))).astype(BF16),
    }
    lw = []
    for l in range(depth):
        lw.append({
            "g1": g_norm1[l].reshape(1, d),
            "w_in": w_in[l].astype(BF16),
            "ws": w_spatial[l].reshape(N_HEADS * CHUNK, CHUNK).astype(BF16),
            "bs": jnp.repeat(b_spatial[l].T, HEAD_DIM, axis=1),
            "wc": w_conv[l],
            "wp": _block_diag(w_pool[l]).astype(BF16),
            "ps": pool_scale[l].reshape(1, GROUP),
            "gg": g_group[l].reshape(1, d),
            "wo_bcd": w_out[l, GROUP:].astype(BF16),
            "wo_a": w_out[l, :GROUP].astype(BF16),
            "wf": _block_diag(w_fourier[l]).astype(BF16),
            "g2": g_norm2[l].reshape(1, d),
        })
    shared["wg"] = w_exp_gate.astype(BF16).reshape(depth * N_EXPERTS, d, D_EXPERT)
    shared["wu"] = w_exp_up.astype(BF16).reshape(depth * N_EXPERTS, d, D_EXPERT)
    shared["wd"] = w_exp_down.astype(BF16).reshape(depth * N_EXPERTS, D_EXPERT, d)

    c_all = jnp.concatenate([c_prompt, c_sample], axis=0)
    mod_all = _ada(c_all, w_ada, b_ada).reshape(depth, c_all.shape[0], N_MOD, d)
    mods_p = [mod_all[l, :nb_p] for l in range(depth)]
    mods_s = [mod_all[l, nb_p:] for l in range(depth)]
    y_prompt = _trunk(x_prompt, mods_p, lw, shared)
    y_sample = _trunk(x_sample, mods_s, lw, shared)
    return (y_prompt, y_sample)
```
